```python
import jax, jax.numpy as jnp
from jax import lax
import numpy as np

D_MODEL = 1024
BATCH = 32
SEQ = 2048
DEPTH = 1

D_MIX = D_MODEL
D_REC = D_MIX // 2
D_POOL = D_MIX - D_REC
REC_HEAD_DIM = 128
N_REC_HEADS = D_REC // REC_HEAD_DIM
POOL_WINDOWS = (2, 4, 8, 16)
N_POOL_GROUPS = len(POOL_WINDOWS)
POOL_GROUP_DIM = D_POOL // N_POOL_GROUPS
D_IN = 4 * D_REC + D_POOL
D_FF = 4 * D_MODEL
N_MOD = 6
CHUNK = 32
EPS = 1e-6

kernel_name = "hybrid_hgrn2_pool_adaln_block"


def _rmsnorm(x, w):
    xf = x.astype(jnp.float32)
    xf = xf * lax.rsqrt(jnp.mean(xf * xf, axis=-1, keepdims=True) + EPS)
    return xf * w.astype(jnp.float32)


def _hgrn2(q, f_logit, v, g, lb, g_norm_w):
    B, T, _ = q.shape
    H, Dh, C = N_REC_HEADS, REC_HEAD_DIM, CHUNK
    n = T // C
    f32 = jnp.float32
    forget = lb + (1.0 - lb) * jax.nn.sigmoid(f_logit.astype(f32))
    k = 1.0 - forget
    logf = jnp.log(forget)
    qs = jax.nn.silu(q.astype(f32)) * (Dh ** -0.5)

    def split(a):
        return a.reshape(B, n, C, H, Dh).transpose(0, 3, 1, 2, 4)

    qs, k, vv, logf = (split(a) for a in (qs, k, v.astype(f32), logf))
    b = jnp.cumsum(logf, axis=3)
    b_ref = b[:, :, :, C // 2 - 1:C // 2]
    b_last = b[:, :, :, C - 1:C]

    scores = jnp.einsum('bhncd,bhnsd->bhncs', qs * jnp.exp(b - b_ref), k * jnp.exp(b_ref - b))
    causal = jnp.tril(jnp.ones((C, C), dtype=bool))
    scores = jnp.where(causal, scores, 0.0)
    o_intra = jnp.einsum('bhncs,bhnse->bhnce', scores, vv)

    q_in = qs * jnp.exp(b)
    k_out = k * jnp.exp(b_last - b)
    decay_chunk = jnp.exp(b_last[:, :, :, 0])

    def step(S, xs):
        q_c, k_c, v_c, d_c = xs
        o = jnp.einsum('bhcd,bhde->bhce', q_c, S)
        S = d_c[..., None] * S + jnp.einsum('bhcd,bhce->bhde', k_c, v_c)
        return S, o

    S0 = jnp.zeros((B, H, Dh, Dh), f32)
    xs = tuple(jnp.moveaxis(a, 2, 0) for a in (q_in, k_out, vv, decay_chunk))
    _, o_inter = lax.scan(step, S0, xs)
    o = o_intra + jnp.moveaxis(o_inter, 0, 2)
    o = o.transpose(0, 2, 3, 1, 4).reshape(B, T, H, Dh)

    gg = g.astype(f32).reshape(B, T, H, Dh)
    o = o * lax.rsqrt(jnp.mean(o * o, axis=-1, keepdims=True) + EPS) * g_norm_w.astype(f32) * jax.nn.silu(gg)
    return o.reshape(B, T, D_REC)


def _pool_mixer(p, w_pool, pool_scale):
    B, T, _ = p.shape
    G, Dg = N_POOL_GROUPS, POOL_GROUP_DIM
    W_MAX = max(POOL_WINDOWS)
    pf = p.astype(jnp.float32)
    cs = jnp.cumsum(pf, axis=1)
    cs_pad = jnp.pad(cs, ((0, 0), (W_MAX, 0), (0, 0)))
    pos = jnp.arange(T)
    outs = []
    for gi, w in enumerate(POOL_WINDOWS):
        sl = slice(gi * Dg, (gi + 1) * Dg)
        lower = cs_pad[:, W_MAX - w:W_MAX - w + T, sl]
        count = jnp.minimum(pos + 1, w).astype(jnp.float32)[None, :, None]
        outs.append((cs[:, :, sl] - lower) / count - pf[:, :, sl])
    pooled = jnp.stack(outs, axis=2)
    mixed = jnp.einsum('btgc,gcd->btgd', pooled, w_pool.astype(jnp.float32))
    return mixed.reshape(B, T, D_POOL) * pool_scale.astype(jnp.float32)


def setup_inputs(seed: int = 0) -> dict:
    key = jax.random.key(seed)
    ks = jax.random.split(key, 16)
    f32 = jnp.float32
    nrm = lambda k, shape, s: jax.random.normal(k, shape, f32) * s
    return {
        "x": nrm(ks[0], (BATCH, SEQ, D_MODEL), 1.0),
        "c": nrm(ks[1], (BATCH, D_MODEL), 1.0),
        "w_ada": nrm(ks[2], (DEPTH, D_MODEL, N_MOD * D_MODEL), 0.5 * D_MODEL ** -0.5),
        "b_ada": nrm(ks[3], (DEPTH, N_MOD * D_MODEL), 0.01),
        "norm_mix_w": 1.0 + nrm(ks[4], (DEPTH, D_MODEL), 0.02),
        "w_in": nrm(ks[5], (DEPTH, D_MODEL, D_IN), D_MODEL ** -0.5),
        "lb_logits": nrm(ks[6], (DEPTH + 1, D_REC), 0.1),
        "g_norm_w": 1.0 + nrm(ks[7], (DEPTH, REC_HEAD_DIM), 0.02),
        "w_pool": nrm(ks[8], (DEPTH, N_POOL_GROUPS, POOL_GROUP_DIM, POOL_GROUP_DIM), POOL_GROUP_DIM ** -0.5),
        "pool_scale": 1.0 + nrm(ks[9], (DEPTH, D_POOL), 0.02),
        "w_out": nrm(ks[10], (DEPTH, D_MIX, D_MODEL), D_MIX ** -0.5),
        "norm_mlp_w": 1.0 + nrm(ks[11], (DEPTH, D_MODEL), 0.02),
        "w_up": nrm(ks[12], (DEPTH, D_MODEL, D_FF), D_MODEL ** -0.5),
        "w_down": nrm(ks[13], (DEPTH, D_FF, D_MODEL), D_FF ** -0.5),
        "norm_final_w": 1.0 + nrm(ks[14], (D_MODEL,), 0.02),
    }


def reference(x, c, w_ada, b_ada, norm_mix_w, w_in, lb_logits, g_norm_w, w_pool, pool_scale,
              w_out, norm_mlp_w, w_up, w_down, norm_final_w):
    dtype = x.dtype
    f32 = jnp.float32
    lbs = jnp.cumsum(jax.nn.softmax(lb_logits.astype(f32), axis=0), axis=0)
    c_act = jax.nn.silu(c.astype(f32))
    h = x.astype(f32)
    for l in range(DEPTH):
        mod = c_act @ w_ada[l].astype(f32) + b_ada[l].astype(f32)
        sh_a, sc_a, gt_a, sh_m, sc_m, gt_m = (m[:, None, :] for m in jnp.split(mod, N_MOD, axis=-1))

        u = _rmsnorm(h, norm_mix_w[l]) * (1.0 + sc_a) + sh_a
        proj = u @ w_in[l].astype(f32)
        q, f_logit, v, g, p = jnp.split(proj, [D_REC, 2 * D_REC, 3 * D_REC, 4 * D_REC], axis=-1)
        o_rec = _hgrn2(q, f_logit, v, g, lbs[l], g_norm_w[l])
        o_pool = _pool_mixer(p, w_pool[l], pool_scale[l])
        mix = jnp.concatenate([o_rec, o_pool], axis=-1) @ w_out[l].astype(f32)
        h = h + gt_a * mix

        u = _rmsnorm(h, norm_mlp_w[l]) * (1.0 + sc_m) + sh_m
        hid = jnp.square(jax.nn.relu(u @ w_up[l].astype(f32)))
        h = h + gt_m * (hid @ w_down[l].astype(f32))
    return _rmsnorm(h, norm_final_w).astype(dtype)
```

```python
import functools

import jax
import jax.numpy as jnp
from jax import lax
from jax.experimental import pallas as pl
from jax.experimental.pallas import tpu as pltpu

F32 = jnp.float32
BF16 = jnp.bfloat16

D_MODEL = 1024
D_REC = 512
D_POOL = 512
HEAD_DIM = 128
N_HEADS = D_REC // HEAD_DIM
POOL_WINDOWS = (2, 4, 8, 16)
POOL_GROUP_DIM = D_POOL // len(POOL_WINDOWS)
W_MAX = max(POOL_WINDOWS)
D_IN = 4 * D_REC + D_POOL
D_FF = 4 * D_MODEL
N_MOD = 6
CHUNK = 32
EPS = 1e-6

TIME_TILE = 256
MOD_COL_BLOCK = 1024
VMEM_LIMIT_BYTES = 56 * 1024 * 1024


def _dot(a, b):
    return jnp.dot(a, b, preferred_element_type=F32)


def _dot_nt(a, b):
    return lax.dot_general(a, b, (((1,), (1,)), ((), ())), preferred_element_type=F32)


def _sigmoid(x):
    return 1.0 / (1.0 + jnp.exp(-x))


def _rmsnorm(x, w):
    return x * lax.rsqrt(jnp.mean(x * x, axis=-1, keepdims=True) + EPS) * w


def _mod_kernel(c_ref, w_ref, b_ref, o_ref):
    c = c_ref[...]
    c_act = (c * _sigmoid(c)).astype(BF16)
    o_ref[...] = _dot(c_act, w_ref[...].astype(BF16)) + b_ref[...]


def _split3(x):
    hi = x.astype(BF16)
    r = x - hi.astype(F32)
    mid = r.astype(BF16)
    lo = (r - mid.astype(F32)).astype(BF16)
    return hi, mid, lo


def _layer_kernel(x_ref, mod_ref, nmix_ref, win_ref, lbl_ref, gnw_ref, wpool_ref, pscale_ref,
                  wout_ref, nmlp_ref, wup_ref, wdown_ref, nfin_ref, o_ref,
                  state_ref, pbuf_ref, hid_ref, *, tt):
    n_chunks = tt // CHUNK
    t_idx = pl.program_id(1)

    @pl.when(t_idx == 0)
    def _():
        state_ref[...] = jnp.zeros_like(state_ref)
        pbuf_ref[0:W_MAX, :] = jnp.zeros((W_MAX, D_POOL), F32)

    x = x_ref[...]
    mod = mod_ref[...]
    sh_a, sc_a, gt_a, sh_m, sc_m, gt_m = (mod[i:i + 1, :] for i in range(N_MOD))

    u = (_rmsnorm(x, nmix_ref[...]) * (1.0 + sc_a) + sh_a).astype(BF16)
    q = _dot(u, win_ref[:, 0 * D_REC:1 * D_REC])
    f_logit = _dot(u, win_ref[:, 1 * D_REC:2 * D_REC])
    v = _dot(u, win_ref[:, 2 * D_REC:3 * D_REC])
    g = _dot(u, win_ref[:, 3 * D_REC:4 * D_REC])
    p = _dot(u, win_ref[:, 4 * D_REC:4 * D_REC + D_POOL])

    lbl = lbl_ref[...]
    lbe = jnp.exp(lbl - jnp.max(lbl, axis=0, keepdims=True))
    lb = lbe[0:1, :] / jnp.sum(lbe, axis=0, keepdims=True)

    forget = lb + (1.0 - lb) * _sigmoid(f_logit)
    k = 1.0 - forget
    logf = jnp.log(forget)
    qs = q * _sigmoid(q) * (HEAD_DIM ** -0.5)

    row = lax.broadcasted_iota(jnp.int32, (tt, tt), 0)
    col = lax.broadcasted_iota(jnp.int32, (tt, tt), 1)
    causal = (row // CHUNK == col // CHUNK) & (col <= row)
    tri = jnp.where(causal, 1.0, 0.0).astype(BF16)
    hi, mid, lo = _split3(logf)
    b = _dot(tri, hi) + _dot(tri, mid) + _dot(tri, lo)

    b3 = b.reshape(n_chunks, CHUNK, D_REC)
    b_ref = b3[:, CHUNK // 2 - 1:CHUNK // 2, :]
    b_last = b3[:, CHUNK - 1:CHUNK, :]
    qs3 = qs.reshape(n_chunks, CHUNK, D_REC)
    k3 = k.reshape(n_chunks, CHUNK, D_REC)
    q_t = (qs3 * jnp.exp(b3 - b_ref)).reshape(tt, D_REC).astype(BF16)
    k_t = (k3 * jnp.exp(b_ref - b3)).reshape(tt, D_REC).astype(BF16)
    q_in = (qs3 * jnp.exp(b3)).reshape(tt, D_REC).astype(BF16)
    k_out = (k3 * jnp.exp(b_last - b3)).reshape(tt, D_REC)
    decay = jnp.exp(b_last).reshape(n_chunks, D_REC)
    v_b = v.astype(BF16)

    row_chunk = lax.broadcasted_iota(jnp.int32, (tt, HEAD_DIM), 0) // CHUNK
    o_heads = []
    for h in range(N_HEADS):
        hs = slice(h * HEAD_DIM, (h + 1) * HEAD_DIM)
        scores = _dot_nt(q_t[:, hs], k_t[:, hs])
        probs = jnp.where(causal, scores, 0.0).astype(BF16)
        o_h = _dot(probs, v_b[:, hs])
        k_out_h = k_out[:, hs]
        k_blk = jnp.concatenate(
            [jnp.where(row_chunk == n, k_out_h, 0.0) for n in range(n_chunks)], axis=1).astype(BF16)
        v_t = v[:, hs].T.astype(BF16)
        u_t = _dot(v_t, k_blk)
        st = state_ref[h]
        o_inter = []
        for n in range(n_chunks):
            rs = slice(n * CHUNK, (n + 1) * CHUNK)
            o_inter.append(_dot_nt(q_in[rs, hs], st.astype(BF16)))
            st = st * decay[n:n + 1, hs] + u_t[:, n * HEAD_DIM:(n + 1) * HEAD_DIM]
        state_ref[h] = st
        o_h = o_h + jnp.concatenate(o_inter, axis=0)
        g_h = g[:, hs]
        o_h = (o_h * lax.rsqrt(jnp.mean(o_h * o_h, axis=-1, keepdims=True) + EPS)
               * gnw_ref[...] * (g_h * _sigmoid(g_h)))
        o_heads.append(o_h)

    pbuf_ref[W_MAX:W_MAX + tt, :] = p
    pos = t_idx * tt + lax.broadcasted_iota(jnp.int32, (tt, POOL_GROUP_DIM), 0)
    for gi, w in enumerate(POOL_WINDOWS):
        cs = slice(gi * POOL_GROUP_DIM, (gi + 1) * POOL_GROUP_DIM)
        win_sum = pbuf_ref[W_MAX:W_MAX + tt, cs]
        for j in range(1, w):
            win_sum = win_sum + pbuf_ref[W_MAX - j:W_MAX - j + tt, cs]
        count = jnp.minimum(pos + 1, w).astype(F32)
        pooled = win_sum * (1.0 / count) - p[:, cs]
        mixed = _dot(pooled.astype(BF16), wpool_ref[gi]) * pscale_ref[:, cs]
        o_heads.append(mixed)
    pbuf_ref[0:W_MAX, :] = pbuf_ref[tt:tt + W_MAX, :]

    mix_in = jnp.concatenate(o_heads, axis=1).astype(BF16)
    h1 = x + gt_a * _dot(mix_in, wout_ref[...])

    u2 = (_rmsnorm(h1, nmlp_ref[...]) * (1.0 + sc_m) + sh_m).astype(BF16)
    for j in range(D_FF // D_MODEL):
        fs = slice(j * D_MODEL, (j + 1) * D_MODEL)
        a = jnp.maximum(_dot(u2, wup_ref[:, fs]), 0.0)
        hid_ref[:, fs] = (a * a).astype(BF16)
    h2 = h1 + gt_m * _dot(hid_ref[...], wdown_ref[...])
    o_ref[...] = _rmsnorm(h2, nfin_ref[...])


def _const_spec(shape):
    zeros = (0,) * len(shape)
    return pl.BlockSpec(shape, lambda *_: zeros, pipeline_mode=pl.Buffered(1))


def kernel(x, c, w_ada, b_ada, norm_mix_w, w_in, lb_logits, g_norm_w, w_pool, pool_scale,
           w_out, norm_mlp_w, w_up, w_down, norm_final_w):
    B, T, D = x.shape
    assert D == D_MODEL and w_ada.shape[0] == 1, "single-layer kernel"
    tt = min(TIME_TILE, T)
    assert T % tt == 0 and tt % CHUNK == 0 and tt >= W_MAX

    n_mod_cols = N_MOD * D
    mod = pl.pallas_call(
        _mod_kernel,
        grid=(n_mod_cols // MOD_COL_BLOCK,),
        in_specs=[
            pl.BlockSpec((B, D), lambda j: (0, 0)),
            pl.BlockSpec((D, MOD_COL_BLOCK), lambda j: (0, j)),
            pl.BlockSpec((1, MOD_COL_BLOCK), lambda j: (0, j)),
        ],
        out_specs=pl.BlockSpec((B, MOD_COL_BLOCK), lambda j: (0, j)),
        out_shape=jax.ShapeDtypeStruct((B, n_mod_cols), F32),
        name="adaln_mod",
    )(c, w_ada[0], b_ada[0].reshape(1, n_mod_cols))
    mod = mod.reshape(B, N_MOD, D)

    row = lambda a: a.reshape(1, -1)
    out = pl.pallas_call(
        functools.partial(_layer_kernel, tt=tt),
        grid=(B, T // tt),
        in_specs=[
            pl.BlockSpec((None, tt, D), lambda b, t: (b, t, 0)),
            pl.BlockSpec((None, N_MOD, D), lambda b, t: (b, 0, 0)),
            _const_spec((1, D)),
            _const_spec((D, D_IN)),
            _const_spec(lb_logits.shape),
            _const_spec((1, HEAD_DIM)),
            _const_spec((len(POOL_WINDOWS), POOL_GROUP_DIM, POOL_GROUP_DIM)),
            _const_spec((1, D_POOL)),
            _const_spec((D, D)),
            _const_spec((1, D)),
            _const_spec((D, D_FF)),
            _const_spec((D_FF, D)),
            _const_spec((1, D)),
        ],
        out_specs=pl.BlockSpec((None, tt, D), lambda b, t: (b, t, 0)),
        out_shape=jax.ShapeDtypeStruct((B, T, D), x.dtype),
        scratch_shapes=[
            pltpu.VMEM((N_HEADS, HEAD_DIM, HEAD_DIM), F32),
            pltpu.VMEM((W_MAX + tt, D_POOL), F32),
            pltpu.VMEM((tt, D_FF), BF16),
        ],
        compiler_params=pltpu.CompilerParams(
            dimension_semantics=("arbitrary", "arbitrary"),
            vmem_limit_bytes=VMEM_LIMIT_BYTES,
        ),
        name="hybrid_layer",
    )(x, mod, row(norm_mix_w[0]), w_in[0].astype(BF16), lb_logits, row(g_norm_w[0]),
      w_pool[0].astype(BF16), row(pool_scale[0]), w_out[0].astype(BF16), row(norm_mlp_w[0]),
      w_up[0].astype(BF16), w_down[0].astype(BF16), row(norm_final_w))
    return out
```

```python
import functools

import jax
import jax.numpy as jnp
from jax import lax
from jax.experimental import pallas as pl
from jax.experimental.pallas import tpu as pltpu

F32 = jnp.float32
BF16 = jnp.bfloat16

D_MODEL = 1024
D_REC = 512
D_POOL = 512
HEAD_DIM = 128
N_HEADS = D_REC // HEAD_DIM
POOL_WINDOWS = (2, 4, 8, 16)
POOL_GROUP_DIM = D_POOL // len(POOL_WINDOWS)
W_MAX = max(POOL_WINDOWS)
D_IN = 4 * D_REC + D_POOL
D_FF = 4 * D_MODEL
N_MOD = 6
CHUNK = 32
EPS = 1e-6

TIME_TILE = 256
DOT_COLS = 512
N_UP = D_FF // DOT_COLS
N_HALF = D_MODEL // DOT_COLS
MOD_COL_BLOCK = 1024
VMEM_LIMIT_BYTES = 56 * 1024 * 1024


def _dot(a, b):
    return jnp.dot(a, b, preferred_element_type=F32)


def _dot_nt(a, b):
    return lax.dot_general(a, b, (((1,), (1,)), ((), ())), preferred_element_type=F32)


def _sigmoid(x):
    return 1.0 / (1.0 + jnp.exp(-x))


def _rms_scale(parts):
    n = sum(a.shape[-1] for a in parts)
    ss = sum(jnp.sum(a * a, axis=-1, keepdims=True) for a in parts)
    return lax.rsqrt(ss * (1.0 / n) + EPS)


def _mod_kernel(c_ref, w_ref, b_ref, o_ref):
    c = c_ref[...]
    c_act = (c * _sigmoid(c)).astype(BF16)
    o_ref[...] = _dot(c_act, w_ref[...].astype(BF16)) + b_ref[...]


def _split3(x):
    hi = x.astype(BF16)
    r = x - hi.astype(F32)
    mid = r.astype(BF16)
    lo = (r - mid.astype(F32)).astype(BF16)
    return hi, mid, lo


def _chunk_block_diag(a, n_chunks):
    rows, width = a.shape
    c = rows // n_chunks
    out = []
    for n in range(n_chunks):
        parts = []
        if n > 0:
            parts.append(jnp.zeros((c, n * width), a.dtype))
        parts.append(a[n * c:(n + 1) * c, :])
        if n < n_chunks - 1:
            parts.append(jnp.zeros((c, (n_chunks - 1 - n) * width), a.dtype))
        out.append(jnp.concatenate(parts, axis=1))
    return jnp.concatenate(out, axis=0)


def _layer_kernel(x_ref, mod_mix_ref, mod_down_ref, nmix_ref, win_ref, lbl_ref, gnw_ref, wpool_ref,
                  pscale_ref, wout_ref, nmlp_ref, wup_ref, wdown_ref, nfin_ref, o_ref,
                  state_ref, pbuf_ref, hid_ref, h1_ref, u2_ref, *, tt, tiles_per_seq, n_tiles):
    n_chunks = tt // CHUNK
    step = pl.program_id(0)
    t_idx = jnp.minimum(step, n_tiles - 1) % tiles_per_seq
    h1_w, h1_r = step % 3, (step + 1) % 3
    u2_w, u2_r = step % 2, (step + 1) % 2
    hid_w, hid_r = (step + 1) % 2, step % 2

    @pl.when(step == 0)
    def _():
        h1_ref[...] = jnp.zeros_like(h1_ref)
        u2_ref[...] = jnp.zeros_like(u2_ref)
        hid_ref[...] = jnp.zeros_like(hid_ref)

    @pl.when(t_idx == 0)
    def _():
        state_ref[...] = jnp.zeros_like(state_ref)
        pbuf_ref[0:W_MAX, :] = jnp.zeros((W_MAX, D_POOL), F32)

    mod = mod_mix_ref[...]
    sh_a, sc_a, gt_a, sh_m, sc_m = (mod[i:i + 1, :] for i in range(5))
    gt_m_down = mod_down_ref[5:6, :]

    u2_prev = u2_ref[u2_r]

    def up_proj(j):
        fs = slice(j * DOT_COLS, (j + 1) * DOT_COLS)
        a = jnp.maximum(_dot(u2_prev, wup_ref[:, fs]), 0.0)
        hid_ref[hid_w, :, fs] = (a * a).astype(BF16)

    hid_prev = hid_ref[hid_r]

    def down_proj(j):
        cs = slice(j * DOT_COLS, (j + 1) * DOT_COLS)
        return h1_ref[h1_r, :, cs] + gt_m_down[:, cs] * _dot(hid_prev, wdown_ref[:, cs])

    h2 = [down_proj(0)]
    x = x_ref[...]
    u = (x * _rms_scale([x]) * nmix_ref[...] * (1.0 + sc_a) + sh_a).astype(BF16)

    q = _dot(u, win_ref[:, 0 * D_REC:1 * D_REC])
    f_logit = _dot(u, win_ref[:, 1 * D_REC:2 * D_REC])

    up_proj(0)
    up_proj(1)
    lbl = lbl_ref[...]
    lbe = jnp.exp(lbl - jnp.max(lbl, axis=0, keepdims=True))
    lb = lbe[0:1, :] / jnp.sum(lbe, axis=0, keepdims=True)
    forget = lb + (1.0 - lb) * _sigmoid(f_logit)
    k = 1.0 - forget
    logf = jnp.log(forget)
    qs = q * _sigmoid(q) * (HEAD_DIM ** -0.5)
    row = lax.broadcasted_iota(jnp.int32, (tt, tt), 0)
    col = lax.broadcasted_iota(jnp.int32, (tt, tt), 1)
    causal = (row // CHUNK == col // CHUNK) & (col <= row)
    tri = jnp.where(causal, 1.0, 0.0).astype(BF16)
    hi, mid, lo = _split3(logf)

    b = _dot(tri, hi) + _dot(tri, mid) + _dot(tri, lo)
    v = _dot(u, win_ref[:, 2 * D_REC:3 * D_REC])
    g = _dot(u, win_ref[:, 3 * D_REC:4 * D_REC])
    p = _dot(u, win_ref[:, 4 * D_REC:4 * D_REC + D_POOL])

    up_proj(2)
    up_proj(3)
    b3 = b.reshape(n_chunks, CHUNK, D_REC)
    b_ref = b3[:, CHUNK // 2 - 1:CHUNK // 2, :]
    b_last = b3[:, CHUNK - 1:CHUNK, :]
    qs3 = qs.reshape(n_chunks, CHUNK, D_REC)
    k3 = k.reshape(n_chunks, CHUNK, D_REC)
    q_t = (qs3 * jnp.exp(b3 - b_ref)).reshape(tt, D_REC).astype(BF16)
    k_t = (k3 * jnp.exp(b_ref - b3)).reshape(tt, D_REC).astype(BF16)
    q_in = (qs3 * jnp.exp(b3)).reshape(tt, D_REC).astype(BF16)
    k_out = (k3 * jnp.exp(b_last - b3)).reshape(tt, D_REC)
    decay = jnp.exp(b_last).reshape(n_chunks, D_REC)
    v_b = v.astype(BF16)

    o_heads = []
    for h in range(N_HEADS):
        hs = slice(h * HEAD_DIM, (h + 1) * HEAD_DIM)
        scores = _dot_nt(q_t[:, hs], k_t[:, hs])
        probs = jnp.where(causal, scores, 0.0).astype(BF16)
        o_h = _dot(probs, v_b[:, hs])
        k_out_t = k_out[:, hs].T.astype(BF16)
        u_all = _dot(k_out_t, _chunk_block_diag(v_b[:, hs], n_chunks))
        decay_col = decay[:, hs].T
        st = state_ref[h]
        o_inter = []
        for n in range(n_chunks):
            rs = slice(n * CHUNK, (n + 1) * CHUNK)
            o_inter.append(_dot(q_in[rs, hs], st.astype(BF16)))
            st = st * decay_col[:, n:n + 1] + u_all[:, n * HEAD_DIM:(n + 1) * HEAD_DIM]
        state_ref[h] = st
        if h % 2 == 1:
            up_proj(4 + h // 2)
        o_h = o_h + jnp.concatenate(o_inter, axis=0)
        g_h = g[:, hs]
        o_h = o_h * _rms_scale([o_h]) * gnw_ref[...] * (g_h * _sigmoid(g_h))
        o_heads.append(o_h)

    h2.append(down_proj(1))
    pbuf_ref[W_MAX:W_MAX + tt, :] = p
    pos = t_idx * tt + lax.broadcasted_iota(jnp.int32, (tt, POOL_GROUP_DIM), 0)
    for gi, w in enumerate(POOL_WINDOWS):
        cs = slice(gi * POOL_GROUP_DIM, (gi + 1) * POOL_GROUP_DIM)
        win_sum = pbuf_ref[:, cs]
        span = 1
        while span < w:
            win_sum = win_sum + pltpu.roll(win_sum, span, 0)
            span *= 2
        count = jnp.minimum(pos + 1, w).astype(F32)
        pooled = win_sum[W_MAX:, :] * (1.0 / count) - p[:, cs]
        o_heads.append(_dot(pooled.astype(BF16), wpool_ref[gi]) * pscale_ref[:, cs])
    pbuf_ref[0:W_MAX, :] = pbuf_ref[tt:tt + W_MAX, :]

    scale = _rms_scale(h2)
    for j in range(N_HALF):
        cs = slice(j * DOT_COLS, (j + 1) * DOT_COLS)
        o_ref[:, cs] = h2[j] * scale * nfin_ref[:, cs]

    mix_in = jnp.concatenate(o_heads, axis=1).astype(BF16)
    h1 = []
    for j in range(N_HALF):
        cs = slice(j * DOT_COLS, (j + 1) * DOT_COLS)
        h1.append(x[:, cs] + gt_a[:, cs] * _dot(mix_in, wout_ref[:, cs]))
        h1_ref[h1_w, :, cs] = h1[j]
    up_proj(6)
    up_proj(7)
    scale = _rms_scale(h1)
    for j in range(N_HALF):
        cs = slice(j * DOT_COLS, (j + 1) * DOT_COLS)
        u2_ref[u2_w, :, cs] = (h1[j] * scale * nmlp_ref[:, cs] * (1.0 + sc_m[:, cs]) + sh_m[:, cs]).astype(BF16)


def _const_spec(shape):
    zeros = (0,) * len(shape)
    return pl.BlockSpec(shape, lambda *_: zeros, pipeline_mode=pl.Buffered(1))


def kernel(x, c, w_ada, b_ada, norm_mix_w, w_in, lb_logits, g_norm_w, w_pool, pool_scale,
           w_out, norm_mlp_w, w_up, w_down, norm_final_w):
    B, T, D = x.shape
    assert D == D_MODEL and w_ada.shape[0] == 1, "single-layer kernel"
    tt = min(TIME_TILE, T)
    assert T % tt == 0 and tt % CHUNK == 0 and tt >= W_MAX
    tiles_per_seq = T // tt
    n_tiles = B * tiles_per_seq

    n_mod_cols = N_MOD * D
    mod = pl.pallas_call(
        _mod_kernel,
        grid=(n_mod_cols // MOD_COL_BLOCK,),
        in_specs=[
            pl.BlockSpec((B, D), lambda j: (0, 0)),
            pl.BlockSpec((D, MOD_COL_BLOCK), lambda j: (0, j)),
            pl.BlockSpec((1, MOD_COL_BLOCK), lambda j: (0, j)),
        ],
        out_specs=pl.BlockSpec((B, MOD_COL_BLOCK), lambda j: (0, j)),
        out_shape=jax.ShapeDtypeStruct((B, n_mod_cols), F32),
        name="adaln_mod",
    )(c, w_ada[0], b_ada[0].reshape(1, n_mod_cols))
    mod = mod.reshape(B, N_MOD, D)

    def mix_tile(s):
        return jnp.minimum(s, n_tiles - 1)

    def down_tile(s):
        return jnp.maximum(s - 2, 0)

    row = lambda a: a.reshape(1, -1)
    out = pl.pallas_call(
        functools.partial(_layer_kernel, tt=tt, tiles_per_seq=tiles_per_seq, n_tiles=n_tiles),
        grid=(n_tiles + 2,),
        in_specs=[
            pl.BlockSpec((None, tt, D), lambda s: (mix_tile(s) // tiles_per_seq, mix_tile(s) % tiles_per_seq, 0)),
            pl.BlockSpec((None, N_MOD, D), lambda s: (mix_tile(s) // tiles_per_seq, 0, 0)),
            pl.BlockSpec((None, N_MOD, D), lambda s: (down_tile(s) // tiles_per_seq, 0, 0)),
            _const_spec((1, D)),
            _const_spec((D, D_IN)),
            _const_spec(lb_logits.shape),
            _const_spec((1, HEAD_DIM)),
            _const_spec((len(POOL_WINDOWS), POOL_GROUP_DIM, POOL_GROUP_DIM)),
            _const_spec((1, D_POOL)),
            _const_spec((D, D)),
            _const_spec((1, D)),
            _const_spec((D, D_FF)),
            _const_spec((D_FF, D)),
            _const_spec((1, D)),
        ],
        out_specs=pl.BlockSpec((None, tt, D),
                               lambda s: (down_tile(s) // tiles_per_seq, down_tile(s) % tiles_per_seq, 0)),
        out_shape=jax.ShapeDtypeStruct((B, T, D), x.dtype),
        scratch_shapes=[
            pltpu.VMEM((N_HEADS, HEAD_DIM, HEAD_DIM), F32),
            pltpu.VMEM((W_MAX + tt, D_POOL), F32),
            pltpu.VMEM((2, tt, D_FF), BF16),
            pltpu.VMEM((3, tt, D), F32),
            pltpu.VMEM((2, tt, D), BF16),
        ],
        compiler_params=pltpu.CompilerParams(
            dimension_semantics=("arbitrary",),
            vmem_limit_bytes=VMEM_LIMIT_BYTES,
        ),
        name="hybrid_layer",
    )(x, mod, mod, row(norm_mix_w[0]), w_in[0].astype(BF16), lb_logits, row(g_norm_w[0]),
      w_pool[0].astype(BF16), row(pool_scale[0]), w_out[0].astype(BF16), row(norm_mlp_w[0]),
      w_up[0].astype(BF16), w_down[0].astype(BF16), row(norm_final_w))
    return out
```

```python
import functools

import jax
import jax.numpy as jnp
from jax import lax
from jax.experimental import pallas as pl
from jax.experimental.pallas import tpu as pltpu

F32 = jnp.float32
BF16 = jnp.bfloat16

D_MODEL = 1024
D_REC = 512
D_POOL = 512
HEAD_DIM = 128
N_HEADS = D_REC // HEAD_DIM
POOL_WINDOWS = (2, 4, 8, 16)
POOL_GROUP_DIM = D_POOL // len(POOL_WINDOWS)
W_MAX = max(POOL_WINDOWS)
D_IN = 4 * D_REC + D_POOL
D_FF = 4 * D_MODEL
N_MOD = 6
CHUNK = 32
EPS = 1e-6

TIME_TILE = 256
DOT_COLS = 512
N_UP = D_FF // DOT_COLS
N_HALF = D_MODEL // DOT_COLS
MOD_COL_BLOCK = 1024
VMEM_LIMIT_BYTES = 56 * 1024 * 1024


def _dot(a, b):
    return jnp.dot(a, b, preferred_element_type=F32)


def _dot_nt(a, b):
    return lax.dot_general(a, b, (((1,), (1,)), ((), ())), preferred_element_type=F32)


def _sigmoid(x):
    return 1.0 / (1.0 + jnp.exp(-x))


def _rms_scale(parts):
    n = sum(a.shape[-1] for a in parts)
    ss = sum(jnp.sum(a * a, axis=-1, keepdims=True) for a in parts)
    return lax.rsqrt(ss * (1.0 / n) + EPS)


def _mod_kernel(c_ref, w_ref, b_ref, o_ref):
    c = c_ref[...]
    c_act = (c * _sigmoid(c)).astype(BF16)
    o_ref[...] = _dot(c_act, w_ref[...].astype(BF16)) + b_ref[...]


def _split3(x):
    hi = x.astype(BF16)
    r = x - hi.astype(F32)
    mid = r.astype(BF16)
    lo = (r - mid.astype(F32)).astype(BF16)
    return hi, mid, lo


def _scale_chunks(a3, rows):
    n_chunks, c, w = a3.shape
    out = [jnp.zeros((c, w), F32) if r is None else a3[n] * r for n, r in enumerate(rows)]
    return jnp.concatenate(out, axis=0).astype(BF16)


def _layer_kernel(x_ref, mod_mix_ref, mod_down_ref, nmix_ref, win_ref, lbl_ref, gnw_ref, wpool_ref,
                  pscale_ref, wout_ref, nmlp_ref, wup_ref, wdown_ref, nfin_ref, o_ref,
                  state_ref, pbuf_ref, hid_ref, h1_ref, u2_ref, *, tt, tiles_per_seq, n_tiles):
    n_chunks = tt // CHUNK
    n_levels = n_chunks.bit_length() - 1
    step = pl.program_id(0)
    t_idx = jnp.minimum(step, n_tiles - 1) % tiles_per_seq
    h1_w, h1_r = step % 3, (step + 1) % 3
    u2_w, u2_r = step % 2, (step + 1) % 2
    hid_w, hid_r = (step + 1) % 2, step % 2

    @pl.when(step == 0)
    def _():
        h1_ref[...] = jnp.zeros_like(h1_ref)
        u2_ref[...] = jnp.zeros_like(u2_ref)
        hid_ref[...] = jnp.zeros_like(hid_ref)

    @pl.when(t_idx == 0)
    def _():
        state_ref[...] = jnp.zeros_like(state_ref)
        pbuf_ref[0:W_MAX, :] = jnp.zeros((W_MAX, D_POOL), F32)

    mod = mod_mix_ref[...]
    sh_a, sc_a, gt_a, sh_m, sc_m = (mod[i:i + 1, :] for i in range(5))
    gt_m_down = mod_down_ref[5:6, :]

    u2_prev = u2_ref[u2_r]

    def up_proj(j):
        fs = slice(j * DOT_COLS, (j + 1) * DOT_COLS)
        a = jnp.maximum(_dot(u2_prev, wup_ref[:, fs]), 0.0)
        hid_ref[hid_w, :, fs] = (a * a).astype(BF16)

    hid_prev = hid_ref[hid_r]

    def down_proj(j):
        cs = slice(j * DOT_COLS, (j + 1) * DOT_COLS)
        return h1_ref[h1_r, :, cs] + gt_m_down[:, cs] * _dot(hid_prev, wdown_ref[:, cs])

    h2 = [down_proj(0)]
    x = x_ref[...]
    u = (x * _rms_scale([x]) * nmix_ref[...] * (1.0 + sc_a) + sh_a).astype(BF16)

    q = _dot(u, win_ref[:, 0 * D_REC:1 * D_REC])
    f_logit = _dot(u, win_ref[:, 1 * D_REC:2 * D_REC])

    up_proj(0)
    up_proj(1)
    lbl = lbl_ref[...]
    lbe = jnp.exp(lbl - jnp.max(lbl, axis=0, keepdims=True))
    lb = lbe[0:1, :] / jnp.sum(lbe, axis=0, keepdims=True)
    forget = lb + (1.0 - lb) * _sigmoid(f_logit)
    k = 1.0 - forget
    logf = jnp.log(forget)
    qs = q * _sigmoid(q) * (HEAD_DIM ** -0.5)
    row = lax.broadcasted_iota(jnp.int32, (tt, tt), 0)
    col = lax.broadcasted_iota(jnp.int32, (tt, tt), 1)
    row_c, col_c = row // CHUNK, col // CHUNK
    causal = (row_c == col_c) & (col <= row)
    tri = jnp.where(causal, 1.0, 0.0).astype(BF16)
    level_masks = [
        (row_c // (1 << l) == col_c // (1 << l)) & (row_c % (1 << l) >= (1 << l) // 2) & (col_c % (1 << l) < (1 << l) // 2)
        for l in range(1, n_levels)]
    hi, mid, lo = _split3(logf)

    b = _dot(tri, hi) + _dot(tri, mid) + _dot(tri, lo)
    v = _dot(u, win_ref[:, 2 * D_REC:3 * D_REC])
    g = _dot(u, win_ref[:, 3 * D_REC:4 * D_REC])
    p = _dot(u, win_ref[:, 4 * D_REC:4 * D_REC + D_POOL])

    up_proj(2)
    up_proj(3)
    b3 = b.reshape(n_chunks, CHUNK, D_REC)
    b_mid = b3[:, CHUNK // 2 - 1:CHUNK // 2, :]
    b_last = b3[:, CHUNK - 1:CHUNK, :]
    qs3 = qs.reshape(n_chunks, CHUNK, D_REC)
    k3 = k.reshape(n_chunks, CHUNK, D_REC)
    q_d = (qs3 * jnp.exp(b3 - b_mid)).reshape(tt, D_REC).astype(BF16)
    k_d = (k3 * jnp.exp(b_mid - b3)).reshape(tt, D_REC).astype(BF16)
    q_in3 = qs3 * jnp.exp(b3)
    k_out3 = k3 * jnp.exp(b_last - b3)
    c = [jnp.zeros((1, D_REC), F32)]
    for n in range(n_chunks):
        c.append(c[n] + b_last[n])
    ones = jnp.ones((1, D_REC), F32)
    q_lv = [q_in3.reshape(tt, D_REC).astype(BF16)]
    k_lv = [k_out3.reshape(tt, D_REC).astype(BF16)]
    for l in range(2, n_levels + 1):
        size, half = 1 << l, (1 << l) // 2
        q_rows, k_rows = [], []
        for n in range(n_chunks):
            r = (n // size) * size + half
            q_rows.append(jnp.exp(c[n] - c[r]) if n % size >= half else None)
            k_rows.append(jnp.exp(c[r] - c[n + 1]) if n % size < half else None)
        q_lv.append(_scale_chunks(q_in3, q_rows))
        k_lv.append(_scale_chunks(k_out3, k_rows))
    q_st = _scale_chunks(q_in3, [ones] + [jnp.exp(c[n]) for n in range(1, n_chunks)])
    k_st = _scale_chunks(k_out3, [jnp.exp(c[n_chunks] - c[n + 1]) for n in range(n_chunks - 1)] + [ones])
    tile_decay = jnp.exp(c[n_chunks])
    v_b = v.astype(BF16)

    o_heads = []
    for h in range(N_HEADS):
        hs = slice(h * HEAD_DIM, (h + 1) * HEAD_DIM)
        scores = _dot_nt(q_lv[-1][:, hs], k_lv[-1][:, hs])
        for l in range(n_levels - 2, -1, -1):
            scores = jnp.where(level_masks[l], _dot_nt(q_lv[l][:, hs], k_lv[l][:, hs]), scores)
        scores = jnp.where(causal, _dot_nt(q_d[:, hs], k_d[:, hs]), scores)
        st_t = state_ref[h]
        o_h = _dot(scores.astype(BF16), v_b[:, hs]) + _dot_nt(q_st[:, hs], st_t.astype(BF16))
        v_t = v[:, hs].T.astype(BF16)
        state_ref[h] = st_t * tile_decay[:, hs] + _dot(v_t, k_st[:, hs])
        if h % 2 == 1:
            up_proj(4 + h // 2)
        g_h = g[:, hs]
        o_h = o_h * _rms_scale([o_h]) * gnw_ref[...] * (g_h * _sigmoid(g_h))
        o_heads.append(o_h)

    h2.append(down_proj(1))
    pbuf_ref[W_MAX:W_MAX + tt, :] = p
    pos = t_idx * tt + lax.broadcasted_iota(jnp.int32, (tt, POOL_GROUP_DIM), 0)
    for gi, w in enumerate(POOL_WINDOWS):
        cs = slice(gi * POOL_GROUP_DIM, (gi + 1) * POOL_GROUP_DIM)
        win_sum = pbuf_ref[:, cs]
        span = 1
        while span < w:
            win_sum = win_sum + pltpu.roll(win_sum, span, 0)
            span *= 2
        count = jnp.minimum(pos + 1, w).astype(F32)
        pooled = win_sum[W_MAX:, :] * (1.0 / count) - p[:, cs]
        o_heads.append(_dot(pooled.astype(BF16), wpool_ref[gi]) * pscale_ref[:, cs])
    pbuf_ref[0:W_MAX, :] = pbuf_ref[tt:tt + W_MAX, :]

    scale = _rms_scale(h2)
    for j in range(N_HALF):
        cs = slice(j * DOT_COLS, (j + 1) * DOT_COLS)
        o_ref[:, cs] = h2[j] * scale * nfin_ref[:, cs]

    mix_in = jnp.concatenate(o_heads, axis=1).astype(BF16)
    h1 = []
    for j in range(N_HALF):
        cs = slice(j * DOT_COLS, (j + 1) * DOT_COLS)
        h1.append(x[:, cs] + gt_a[:, cs] * _dot(mix_in, wout_ref[:, cs]))
        h1_ref[h1_w, :, cs] = h1[j]
    up_proj(6)
    up_proj(7)
    scale = _rms_scale(h1)
    for j in range(N_HALF):
        cs = slice(j * DOT_COLS, (j + 1) * DOT_COLS)
        u2_ref[u2_w, :, cs] = (h1[j] * scale * nmlp_ref[:, cs] * (1.0 + sc_m[:, cs]) + sh_m[:, cs]).astype(BF16)


def _const_spec(shape):
    zeros = (0,) * len(shape)
    return pl.BlockSpec(shape, lambda *_: zeros, pipeline_mode=pl.Buffered(1))


def kernel(x, c, w_ada, b_ada, norm_mix_w, w_in, lb_logits, g_norm_w, w_pool, pool_scale,
           w_out, norm_mlp_w, w_up, w_down, norm_final_w):
    B, T, D = x.shape
    assert D == D_MODEL and w_ada.shape[0] == 1, "single-layer kernel"
    tt = min(TIME_TILE, T)
    n_chunks = tt // CHUNK
    assert T % tt == 0 and tt % CHUNK == 0 and tt >= W_MAX and n_chunks & (n_chunks - 1) == 0 and n_chunks >= 4
    tiles_per_seq = T // tt
    n_tiles = B * tiles_per_seq

    n_mod_cols = N_MOD * D
    mod = pl.pallas_call(
        _mod_kernel,
        grid=(n_mod_cols // MOD_COL_BLOCK,),
        in_specs=[
            pl.BlockSpec((B, D), lambda j: (0, 0)),
            pl.BlockSpec((D, MOD_COL_BLOCK), lambda j: (0, j)),
            pl.BlockSpec((1, MOD_COL_BLOCK), lambda j: (0, j)),
        ],
        out_specs=pl.BlockSpec((B, MOD_COL_BLOCK), lambda j: (0, j)),
        out_shape=jax.ShapeDtypeStruct((B, n_mod_cols), F32),
        name="adaln_mod",
    )(c, w_ada[0], b_ada[0].reshape(1, n_mod_cols))
    mod = mod.reshape(B, N_MOD, D)

    def mix_tile(s):
        return jnp.minimum(s, n_tiles - 1)

    def down_tile(s):
        return jnp.maximum(s - 2, 0)

    row = lambda a: a.reshape(1, -1)
    out = pl.pallas_call(
        functools.partial(_layer_kernel, tt=tt, tiles_per_seq=tiles_per_seq, n_tiles=n_tiles),
        grid=(n_tiles + 2,),
        in_specs=[
            pl.BlockSpec((None, tt, D), lambda s: (mix_tile(s) // tiles_per_seq, mix_tile(s) % tiles_per_seq, 0)),
            pl.BlockSpec((None, N_MOD, D), lambda s: (mix_tile(s) // tiles_per_seq, 0, 0)),
            pl.BlockSpec((None, N_MOD, D), lambda s: (down_tile(s) // tiles_per_seq, 0, 0)),
            _const_spec((1, D)),
            _const_spec((D, D_IN)),
            _const_spec(lb_logits.shape),
            _const_spec((1, HEAD_DIM)),
            _const_spec((len(POOL_WINDOWS), POOL_GROUP_DIM, POOL_GROUP_DIM)),
            _const_spec((1, D_POOL)),
            _const_spec((D, D)),
            _const_spec((1, D)),
            _const_spec((D, D_FF)),
            _const_spec((D_FF, D)),
            _const_spec((1, D)),
        ],
        out_specs=pl.BlockSpec((None, tt, D),
                               lambda s: (down_tile(s) // tiles_per_seq, down_tile(s) % tiles_per_seq, 0)),
        out_shape=jax.ShapeDtypeStruct((B, T, D), x.dtype),
        scratch_shapes=[
            pltpu.VMEM((N_HEADS, HEAD_DIM, HEAD_DIM), F32),
            pltpu.VMEM((W_MAX + tt, D_POOL), F32),
            pltpu.VMEM((2, tt, D_FF), BF16),
            pltpu.VMEM((3, tt, D), F32),
            pltpu.VMEM((2, tt, D), BF16),
        ],
        compiler_params=pltpu.CompilerParams(
            dimension_semantics=("arbitrary",),
            vmem_limit_bytes=VMEM_LIMIT_BYTES,
        ),
        name="hybrid_layer",
    )(x, mod, mod, row(norm_mix_w[0]), w_in[0].astype(BF16), lb_logits, row(g_norm_w[0]),
      w_pool[0].astype(BF16), row(pool_scale[0]), w_out[0].astype(BF16), row(norm_mlp_w[0]),
      w_up[0].astype(BF16), w_down[0].astype(BF16), row(norm_final_w))
    return out
```

```python
import functools

import jax
import jax.numpy as jnp
from jax import lax
from jax.experimental import pallas as pl
from jax.experimental.pallas import tpu as pltpu

F32 = jnp.float32
BF16 = jnp.bfloat16

D_MODEL = 1024
D_REC = 512
D_POOL = 512
HEAD_DIM = 128
N_HEADS = D_REC // HEAD_DIM
POOL_WINDOWS = (2, 4, 8, 16)
POOL_GROUP_DIM = D_POOL // len(POOL_WINDOWS)
W_MAX = max(POOL_WINDOWS)
D_IN = 4 * D_REC + D_POOL
D_FF = 4 * D_MODEL
N_MOD = 6
CHUNK = 32
EPS = 1e-6

TIME_TILE = 512
MIX_ROWS = 256
DOT_COLS = 512
N_UP = D_FF // DOT_COLS
N_HALF = D_MODEL // DOT_COLS
MOD_COL_BLOCK = 1024
VMEM_LIMIT_BYTES = 60 * 1024 * 1024


def _dot(a, b):
    return jnp.dot(a, b, preferred_element_type=F32)


def _dot_nt(a, b):
    return lax.dot_general(a, b, (((1,), (1,)), ((), ())), preferred_element_type=F32)


def _sigmoid(x):
    return 1.0 / (1.0 + jnp.exp(-x))


def _rms_scale(parts):
    n = sum(a.shape[-1] for a in parts)
    ss = sum(jnp.sum(a * a, axis=-1, keepdims=True) for a in parts)
    return lax.rsqrt(ss * (1.0 / n) + EPS)


def _mod_kernel(c_ref, w_ref, b_ref, o_ref):
    c = c_ref[...]
    c_act = (c * _sigmoid(c)).astype(BF16)
    o_ref[...] = _dot(c_act, w_ref[...].astype(BF16)) + b_ref[...]


def _split3(x):
    hi = x.astype(BF16)
    r = x - hi.astype(F32)
    mid = r.astype(BF16)
    lo = (r - mid.astype(F32)).astype(BF16)
    return hi, mid, lo


def _scale_chunks(a3, rows):
    n_chunks, c, w = a3.shape
    out = [jnp.zeros((c, w), F32) if r is None else a3[n] * r for n, r in enumerate(rows)]
    return jnp.concatenate(out, axis=0).astype(BF16)


def _layer_kernel(x_ref, mod_mix_ref, mod_down_ref, nmix_ref, win_ref, lbl_ref, gnw_ref, wpool_ref,
                  pscale_ref, wout_ref, nmlp_ref, wup_ref, wdown_ref, nfin_ref, o_ref,
                  state_ref, pbuf_ref, hid_ref, h1_ref, u2_ref, *, tt, mr, tiles_per_seq, n_tiles):
    n_sub = tt // mr
    n_chunks = mr // CHUNK
    n_levels = n_chunks.bit_length() - 1
    step = pl.program_id(0)
    t_idx = jnp.minimum(step, n_tiles - 1) % tiles_per_seq
    h1_w, h1_r = step % 3, (step + 1) % 3
    u2_w, u2_r = step % 2, (step + 1) % 2
    hid_w, hid_r = (step + 1) % 2, step % 2

    @pl.when(step == 0)
    def _():
        h1_ref[...] = jnp.zeros_like(h1_ref)
        u2_ref[...] = jnp.zeros_like(u2_ref)
        hid_ref[...] = jnp.zeros_like(hid_ref)

    @pl.when(t_idx == 0)
    def _():
        state_ref[...] = jnp.zeros_like(state_ref)
        pbuf_ref[0:W_MAX, :] = jnp.zeros((W_MAX, D_POOL), F32)

    mod = mod_mix_ref[...]
    sh_a, sc_a, gt_a, sh_m, sc_m = (mod[i:i + 1, :] for i in range(5))
    gt_m_down = mod_down_ref[5:6, :]

    u2_prev = u2_ref[u2_r]

    def up_proj(j):
        fs = slice(j * DOT_COLS, (j + 1) * DOT_COLS)
        a = jnp.maximum(_dot(u2_prev, wup_ref[:, fs]), 0.0)
        hid_ref[hid_w, :, fs] = (a * a).astype(BF16)

    hid_prev = hid_ref[hid_r]
    h2 = []

    def down_proj(j):
        cs = slice(j * DOT_COLS, (j + 1) * DOT_COLS)
        h2.append(h1_ref[h1_r, :, cs] + gt_m_down[:, cs] * _dot(hid_prev, wdown_ref[:, cs]))

    def emit_output():
        scale = _rms_scale(h2)
        for j in range(N_HALF):
            cs = slice(j * DOT_COLS, (j + 1) * DOT_COLS)
            o_ref[:, cs] = h2[j] * scale * nfin_ref[:, cs]

    lbl = lbl_ref[...]
    lbe = jnp.exp(lbl - jnp.max(lbl, axis=0, keepdims=True))
    lb = lbe[0:1, :] / jnp.sum(lbe, axis=0, keepdims=True)
    row = lax.broadcasted_iota(jnp.int32, (mr, mr), 0)
    col = lax.broadcasted_iota(jnp.int32, (mr, mr), 1)
    row_c, col_c = row // CHUNK, col // CHUNK
    causal = (row_c == col_c) & (col <= row)
    tri = jnp.where(causal, 1.0, 0.0).astype(BF16)
    level_masks = [
        (row_c // (1 << l) == col_c // (1 << l)) & (row_c % (1 << l) >= (1 << l) // 2) & (col_c % (1 << l) < (1 << l) // 2)
        for l in range(1, n_levels)]
    ones = jnp.ones((1, D_REC), F32)

    def mix(r):
        rows = slice(r * mr, (r + 1) * mr)
        x = x_ref[rows, :]
        u = (x * _rms_scale([x]) * nmix_ref[...] * (1.0 + sc_a) + sh_a).astype(BF16)
        q = _dot(u, win_ref[:, 0 * D_REC:1 * D_REC])
        f_logit = _dot(u, win_ref[:, 1 * D_REC:2 * D_REC])
        yield
        forget = lb + (1.0 - lb) * _sigmoid(f_logit)
        k = 1.0 - forget
        logf = jnp.log(forget)
        qs = q * _sigmoid(q) * (HEAD_DIM ** -0.5)
        hi, mid, lo = _split3(logf)
        b = _dot(tri, hi) + _dot(tri, mid) + _dot(tri, lo)
        v = _dot(u, win_ref[:, 2 * D_REC:3 * D_REC])
        g = _dot(u, win_ref[:, 3 * D_REC:4 * D_REC])
        p = _dot(u, win_ref[:, 4 * D_REC:4 * D_REC + D_POOL])
        yield
        b3 = b.reshape(n_chunks, CHUNK, D_REC)
        b_mid = b3[:, CHUNK // 2 - 1:CHUNK // 2, :]
        b_last = b3[:, CHUNK - 1:CHUNK, :]
        qs3 = qs.reshape(n_chunks, CHUNK, D_REC)
        k3 = k.reshape(n_chunks, CHUNK, D_REC)
        q_d = (qs3 * jnp.exp(b3 - b_mid)).reshape(mr, D_REC).astype(BF16)
        k_d = (k3 * jnp.exp(b_mid - b3)).reshape(mr, D_REC).astype(BF16)
        q_in3 = qs3 * jnp.exp(b3)
        k_out3 = k3 * jnp.exp(b_last - b3)
        c = [jnp.zeros((1, D_REC), F32)]
        for n in range(n_chunks):
            c.append(c[n] + b_last[n])
        q_lv = [q_in3.reshape(mr, D_REC).astype(BF16)]
        k_lv = [k_out3.reshape(mr, D_REC).astype(BF16)]
        for l in range(2, n_levels + 1):
            size, half = 1 << l, (1 << l) // 2
            q_rows, k_rows = [], []
            for n in range(n_chunks):
                mid_b = (n // size) * size + half
                q_rows.append(jnp.exp(c[n] - c[mid_b]) if n % size >= half else None)
                k_rows.append(jnp.exp(c[mid_b] - c[n + 1]) if n % size < half else None)
            q_lv.append(_scale_chunks(q_in3, q_rows))
            k_lv.append(_scale_chunks(k_out3, k_rows))
        q_st = _scale_chunks(q_in3, [ones] + [jnp.exp(c[n]) for n in range(1, n_chunks)])
        k_st = _scale_chunks(k_out3, [jnp.exp(c[n_chunks] - c[n + 1]) for n in range(n_chunks - 1)] + [ones])
        sub_decay = jnp.exp(c[n_chunks])
        v_b = v.astype(BF16)
        o_heads = []
        for h in range(N_HEADS):
            hs = slice(h * HEAD_DIM, (h + 1) * HEAD_DIM)
            scores = _dot_nt(q_lv[-1][:, hs], k_lv[-1][:, hs])
            for l in range(n_levels - 2, -1, -1):
                scores = jnp.where(level_masks[l], _dot_nt(q_lv[l][:, hs], k_lv[l][:, hs]), scores)
            scores = jnp.where(causal, _dot_nt(q_d[:, hs], k_d[:, hs]), scores)
            st_t = state_ref[h]
            o_h = _dot(scores.astype(BF16), v_b[:, hs]) + _dot_nt(q_st[:, hs], st_t.astype(BF16))
            v_t = v[:, hs].T.astype(BF16)
            state_ref[h] = st_t * sub_decay[:, hs] + _dot(v_t, k_st[:, hs])
            g_h = g[:, hs]
            o_heads.append(o_h * _rms_scale([o_h]) * gnw_ref[...] * (g_h * _sigmoid(g_h)))
            if h % 2 == 1:
                yield
        pbuf_ref[W_MAX:W_MAX + mr, :] = p
        pos = t_idx * tt + r * mr + lax.broadcasted_iota(jnp.int32, (mr, POOL_GROUP_DIM), 0)
        for gi, w in enumerate(POOL_WINDOWS):
            cs = slice(gi * POOL_GROUP_DIM, (gi + 1) * POOL_GROUP_DIM)
            win_sum = pbuf_ref[:, cs]
            span = 1
            while span < w:
                win_sum = win_sum + pltpu.roll(win_sum, span, 0)
                span *= 2
            count = jnp.minimum(pos + 1, w).astype(F32)
            pooled = win_sum[W_MAX:, :] * (1.0 / count) - p[:, cs]
            o_heads.append(_dot(pooled.astype(BF16), wpool_ref[gi]) * pscale_ref[:, cs])
        pbuf_ref[0:W_MAX, :] = pbuf_ref[mr:mr + W_MAX, :]
        yield
        mix_in = jnp.concatenate(o_heads, axis=1).astype(BF16)
        h1 = []
        for j in range(N_HALF):
            cs = slice(j * DOT_COLS, (j + 1) * DOT_COLS)
            h1.append(x[:, cs] + gt_a[:, cs] * _dot(mix_in, wout_ref[:, cs]))
            h1_ref[h1_w, rows, cs] = h1[j]
        yield
        scale = _rms_scale(h1)
        for j in range(N_HALF):
            cs = slice(j * DOT_COLS, (j + 1) * DOT_COLS)
            u2_ref[u2_w, rows, cs] = (h1[j] * scale * nmlp_ref[:, cs] * (1.0 + sc_m[:, cs]) + sh_m[:, cs]).astype(BF16)

    n_yields = 6
    schedule = {(r, y): [] for r in range(n_sub) for y in range(n_yields)}
    ups = [functools.partial(up_proj, j) for j in range(N_UP)]
    early = [(r, y) for r in range(n_sub) for y in range(4)]
    for i, piece in enumerate(ups[:-1]):
        schedule[early[(i * len(early)) // (N_UP - 1)]].append(piece)
    schedule[(n_sub - 1, n_yields - 1)].append(ups[-1])
    mid = (n_sub // 2 - 1, n_yields - 1) if n_sub > 1 else (0, 4)
    schedule[mid] += [functools.partial(down_proj, j) for j in range(1, N_HALF)] + [emit_output]
    down_proj(0)
    for r in range(n_sub):
        for y, _ in enumerate(mix(r)):
            for piece in schedule[(r, y)]:
                piece()


def _const_spec(shape):
    zeros = (0,) * len(shape)
    return pl.BlockSpec(shape, lambda *_: zeros, pipeline_mode=pl.Buffered(1))


def kernel(x, c, w_ada, b_ada, norm_mix_w, w_in, lb_logits, g_norm_w, w_pool, pool_scale,
           w_out, norm_mlp_w, w_up, w_down, norm_final_w):
    B, T, D = x.shape
    assert D == D_MODEL and w_ada.shape[0] == 1, "single-layer kernel"
    tt = min(TIME_TILE, T)
    mr = min(MIX_ROWS, tt)
    n_chunks = mr // CHUNK
    assert T % tt == 0 and tt % mr == 0 and mr % CHUNK == 0 and mr >= W_MAX
    assert n_chunks & (n_chunks - 1) == 0 and n_chunks >= 4
    tiles_per_seq = T // tt
    n_tiles = B * tiles_per_seq

    n_mod_cols = N_MOD * D
    mod = pl.pallas_call(
        _mod_kernel,
        grid=(n_mod_cols // MOD_COL_BLOCK,),
        in_specs=[
            pl.BlockSpec((B, D), lambda j: (0, 0)),
            pl.BlockSpec((D, MOD_COL_BLOCK), lambda j: (0, j)),
            pl.BlockSpec((1, MOD_COL_BLOCK), lambda j: (0, j)),
        ],
        out_specs=pl.BlockSpec((B, MOD_COL_BLOCK), lambda j: (0, j)),
        out_shape=jax.ShapeDtypeStruct((B, n_mod_cols), F32),
        name="adaln_mod",
    )(c, w_ada[0], b_ada[0].reshape(1, n_mod_cols))
    mod = mod.reshape(B, N_MOD, D)

    def mix_tile(s):
        return jnp.minimum(s, n_tiles - 1)

    def down_tile(s):
        return jnp.maximum(s - 2, 0)

    row = lambda a: a.reshape(1, -1)
    out = pl.pallas_call(
        functools.partial(_layer_kernel, tt=tt, mr=mr, tiles_per_seq=tiles_per_seq, n_tiles=n_tiles),
        grid=(n_tiles + 2,),
        in_specs=[
            pl.BlockSpec((None, tt, D), lambda s: (mix_tile(s) // tiles_per_seq, mix_tile(s) % tiles_per_seq, 0)),
            pl.BlockSpec((None, N_MOD, D), lambda s: (mix_tile(s) // tiles_per_seq, 0, 0)),
            pl.BlockSpec((None, N_MOD, D), lambda s: (down_tile(s) // tiles_per_seq, 0, 0)),
            _const_spec((1, D)),
            _const_spec((D, D_IN)),
            _const_spec(lb_logits.shape),
            _const_spec((1, HEAD_DIM)),
            _const_spec((len(POOL_WINDOWS), POOL_GROUP_DIM, POOL_GROUP_DIM)),
            _const_spec((1, D_POOL)),
            _const_spec((D, D)),
            _const_spec((1, D)),
            _const_spec((D, D_FF)),
            _const_spec((D_FF, D)),
            _const_spec((1, D)),
        ],
        out_specs=pl.BlockSpec((None, tt, D),
                               lambda s: (down_tile(s) // tiles_per_seq, down_tile(s) % tiles_per_seq, 0)),
        out_shape=jax.ShapeDtypeStruct((B, T, D), x.dtype),
        scratch_shapes=[
            pltpu.VMEM((N_HEADS, HEAD_DIM, HEAD_DIM), F32),
            pltpu.VMEM((W_MAX + mr, D_POOL), F32),
            pltpu.VMEM((2, tt, D_FF), BF16),
            pltpu.VMEM((3, tt, D), F32),
            pltpu.VMEM((2, tt, D), BF16),
        ],
        compiler_params=pltpu.CompilerParams(
            dimension_semantics=("arbitrary",),
            vmem_limit_bytes=VMEM_LIMIT_BYTES,
        ),
        name="hybrid_layer",
    )(x, mod, mod, row(norm_mix_w[0]), w_in[0].astype(BF16), lb_logits, row(g_norm_w[0]),
      w_pool[0].astype(BF16), row(pool_scale[0]), w_out[0].astype(BF16), row(norm_mlp_w[0]),
      w_up[0].astype(BF16), w_down[0].astype(BF16), row(norm_final_w))
    return out
```

```python
import functools

import jax
import jax.numpy as jnp
from jax import lax
from jax.experimental import pallas as pl
from jax.experimental.pallas import tpu as pltpu

F32 = jnp.float32
BF16 = jnp.bfloat16

D_MODEL = 1024
D_REC = 512
D_POOL = 512
HEAD_DIM = 128
N_HEADS = D_REC // HEAD_DIM
POOL_WINDOWS = (2, 4, 8, 16)
POOL_GROUP_DIM = D_POOL // len(POOL_WINDOWS)
W_MAX = max(POOL_WINDOWS)
D_IN = 4 * D_REC + D_POOL
D_FF = 4 * D_MODEL
N_MOD = 6
CHUNK = 32
EPS = 1e-6

TIME_TILE = 512
MIX_ROWS = 256
DOT_COLS = 512
N_IN = D_IN // DOT_COLS
N_UP = D_FF // DOT_COLS
N_HALF = D_MODEL // DOT_COLS
MOD_COL_BLOCK = 1024
VMEM_LIMIT_BYTES = 62 * 1024 * 1024


def _dot(a, b):
    return jnp.dot(a, b, preferred_element_type=F32)


def _dot_nt(a, b):
    return lax.dot_general(a, b, (((1,), (1,)), ((), ())), preferred_element_type=F32)


def _sigmoid(x):
    return 1.0 / (1.0 + jnp.exp(-x))


def _rms_scale(parts):
    n = sum(a.shape[-1] for a in parts)
    ss = sum(jnp.sum(a * a, axis=-1, keepdims=True) for a in parts)
    return lax.rsqrt(ss * (1.0 / n) + EPS)


def _mod_kernel(c_ref, w_ref, b_ref, o_ref):
    c = c_ref[...]
    c_act = (c * _sigmoid(c)).astype(BF16)
    o_ref[...] = _dot(c_act, w_ref[...].astype(BF16)) + b_ref[...]


def _split3(x):
    hi = x.astype(BF16)
    r = x - hi.astype(F32)
    mid = r.astype(BF16)
    lo = (r - mid.astype(F32)).astype(BF16)
    return hi, mid, lo


def _scale_chunks(a3, rows):
    n_chunks, c, w = a3.shape
    out = [jnp.zeros((c, w), F32) if r is None else a3[n] * r for n, r in enumerate(rows)]
    return jnp.concatenate(out, axis=0).astype(BF16)


def _layer_kernel(x_ref, mod_mix_ref, mod_down_ref, nmix_ref, win_ref, lbl_ref, gnw_ref, wpool_ref,
                  pscale_ref, wout_ref, nmlp_ref, wup_ref, wdown_ref, nfin_ref, o_ref,
                  state_ref, pbuf_ref, hid_ref, h1_ref, u2_ref, proj_ref, mixin_ref,
                  *, tt, mr, tiles_per_seq, n_tiles):
    n_sub = tt // mr
    n_chunks = mr // CHUNK
    n_levels = n_chunks.bit_length() - 1
    step = pl.program_id(0)
    t_idx = jnp.minimum(step, n_tiles - 1) % tiles_per_seq
    h1_s = step % 2
    u2_w, u2_r = step % 2, (step + 1) % 2
    hid_w, hid_r = (step + 1) % 2, step % 2

    @pl.when(step == 0)
    def _():
        h1_ref[...] = jnp.zeros_like(h1_ref)
        u2_ref[...] = jnp.zeros_like(u2_ref)
        hid_ref[...] = jnp.zeros_like(hid_ref)

    @pl.when(t_idx == 0)
    def _():
        state_ref[...] = jnp.zeros_like(state_ref)
        pbuf_ref[0:W_MAX, :] = jnp.zeros((W_MAX, D_POOL), F32)

    mod = mod_mix_ref[...]
    sh_a, sc_a, gt_a, sh_m, sc_m = (mod[i:i + 1, :] for i in range(5))
    gt_m_down = mod_down_ref[5:6, :]

    u2_prev = u2_ref[u2_r]

    def up_proj(j):
        fs = slice(j * DOT_COLS, (j + 1) * DOT_COLS)
        a = jnp.maximum(_dot(u2_prev, wup_ref[:, fs]), 0.0)
        hid_ref[hid_w, :, fs] = (a * a).astype(BF16)

    hid_prev = hid_ref[hid_r]
    h2 = []

    def down_proj(j):
        cs = slice(j * DOT_COLS, (j + 1) * DOT_COLS)
        h2.append(h1_ref[h1_s, :, cs] + gt_m_down[:, cs] * _dot(hid_prev, wdown_ref[:, cs]))

    def emit_output():
        scale = _rms_scale(h2)
        for j in range(N_HALF):
            cs = slice(j * DOT_COLS, (j + 1) * DOT_COLS)
            o_ref[:, cs] = h2[j] * scale * nfin_ref[:, cs]

    def project():
        x = x_ref[...]
        u = (x * _rms_scale([x]) * nmix_ref[...] * (1.0 + sc_a) + sh_a).astype(BF16)
        for j in range(N_IN):
            cs = slice(j * DOT_COLS, (j + 1) * DOT_COLS)
            proj_ref[:, cs] = _dot(u, win_ref[:, cs])
            if j == 1:
                yield
        yield

    lbl = lbl_ref[...]
    lbe = jnp.exp(lbl - jnp.max(lbl, axis=0, keepdims=True))
    lb = lbe[0:1, :] / jnp.sum(lbe, axis=0, keepdims=True)
    row = lax.broadcasted_iota(jnp.int32, (mr, mr), 0)
    col = lax.broadcasted_iota(jnp.int32, (mr, mr), 1)
    row_c, col_c = row // CHUNK, col // CHUNK
    causal = (row_c == col_c) & (col <= row)
    tri = jnp.where(causal, 1.0, 0.0).astype(BF16)
    level_masks = [
        (row_c // (1 << l) == col_c // (1 << l)) & (row_c % (1 << l) >= (1 << l) // 2) & (col_c % (1 << l) < (1 << l) // 2)
        for l in range(1, n_levels)]
    ones = jnp.ones((1, D_REC), F32)

    def mix(r):
        rows = slice(r * mr, (r + 1) * mr)
        q = proj_ref[rows, 0 * D_REC:1 * D_REC]
        f_logit = proj_ref[rows, 1 * D_REC:2 * D_REC]
        forget = lb + (1.0 - lb) * _sigmoid(f_logit)
        k = 1.0 - forget
        logf = jnp.log(forget)
        qs = q * _sigmoid(q) * (HEAD_DIM ** -0.5)
        hi, mid, lo = _split3(logf)
        b = _dot(tri, hi) + _dot(tri, mid) + _dot(tri, lo)
        yield
        b3 = b.reshape(n_chunks, CHUNK, D_REC)
        b_mid = b3[:, CHUNK // 2 - 1:CHUNK // 2, :]
        b_last = b3[:, CHUNK - 1:CHUNK, :]
        qs3 = qs.reshape(n_chunks, CHUNK, D_REC)
        k3 = k.reshape(n_chunks, CHUNK, D_REC)
        q_d = (qs3 * jnp.exp(b3 - b_mid)).reshape(mr, D_REC).astype(BF16)
        k_d = (k3 * jnp.exp(b_mid - b3)).reshape(mr, D_REC).astype(BF16)
        q_in3 = qs3 * jnp.exp(b3)
        k_out3 = k3 * jnp.exp(b_last - b3)
        c = [jnp.zeros((1, D_REC), F32)]
        for n in range(n_chunks):
            c.append(c[n] + b_last[n])
        q_lv = [q_in3.reshape(mr, D_REC).astype(BF16)]
        k_lv = [k_out3.reshape(mr, D_REC).astype(BF16)]
        for l in range(2, n_levels + 1):
            size, half = 1 << l, (1 << l) // 2
            q_rows, k_rows = [], []
            for n in range(n_chunks):
                mid_b = (n // size) * size + half
                q_rows.append(jnp.exp(c[n] - c[mid_b]) if n % size >= half else None)
                k_rows.append(jnp.exp(c[mid_b] - c[n + 1]) if n % size < half else None)
            q_lv.append(_scale_chunks(q_in3, q_rows))
            k_lv.append(_scale_chunks(k_out3, k_rows))
        q_st = _scale_chunks(q_in3, [ones] + [jnp.exp(c[n]) for n in range(1, n_chunks)])
        k_st = _scale_chunks(k_out3, [jnp.exp(c[n_chunks] - c[n + 1]) for n in range(n_chunks - 1)] + [ones])
        sub_decay = jnp.exp(c[n_chunks])
        v = proj_ref[rows, 2 * D_REC:3 * D_REC]
        v_b = v.astype(BF16)
        o_heads = []
        for h in range(N_HEADS):
            hs = slice(h * HEAD_DIM, (h + 1) * HEAD_DIM)
            scores = _dot_nt(q_lv[-1][:, hs], k_lv[-1][:, hs])
            for l in range(n_levels - 2, -1, -1):
                scores = jnp.where(level_masks[l], _dot_nt(q_lv[l][:, hs], k_lv[l][:, hs]), scores)
            scores = jnp.where(causal, _dot_nt(q_d[:, hs], k_d[:, hs]), scores)
            st_t = state_ref[h]
            o_h = _dot(scores.astype(BF16), v_b[:, hs]) + _dot_nt(q_st[:, hs], st_t.astype(BF16))
            v_t = v[:, hs].T.astype(BF16)
            state_ref[h] = st_t * sub_decay[:, hs] + _dot(v_t, k_st[:, hs])
            g_h = proj_ref[rows, 3 * D_REC + h * HEAD_DIM:3 * D_REC + (h + 1) * HEAD_DIM]
            o_heads.append(o_h * _rms_scale([o_h]) * gnw_ref[...] * (g_h * _sigmoid(g_h)))
            if h % 2 == 1:
                yield
        p = proj_ref[rows, 4 * D_REC:4 * D_REC + D_POOL]
        pbuf_ref[W_MAX:W_MAX + mr, :] = p
        pos = t_idx * tt + r * mr + lax.broadcasted_iota(jnp.int32, (mr, POOL_GROUP_DIM), 0)
        for gi, w in enumerate(POOL_WINDOWS):
            cs = slice(gi * POOL_GROUP_DIM, (gi + 1) * POOL_GROUP_DIM)
            win_sum = pbuf_ref[:, cs]
            span = 1
            while span < w:
                win_sum = win_sum + pltpu.roll(win_sum, span, 0)
                span *= 2
            count = jnp.minimum(pos + 1, w).astype(F32)
            pooled = win_sum[W_MAX:, :] * (1.0 / count) - p[:, cs]
            o_heads.append(_dot(pooled.astype(BF16), wpool_ref[gi]) * pscale_ref[:, cs])
        pbuf_ref[0:W_MAX, :] = pbuf_ref[mr:mr + W_MAX, :]
        mixin_ref[rows, :] = jnp.concatenate(o_heads, axis=1).astype(BF16)
        yield

    def finish():
        x = x_ref[...]
        mix_in = mixin_ref[...]
        h1 = []
        for j in range(N_HALF):
            cs = slice(j * DOT_COLS, (j + 1) * DOT_COLS)
            h1.append(x[:, cs] + gt_a[:, cs] * _dot(mix_in, wout_ref[:, cs]))
            h1_ref[h1_s, :, cs] = h1[j]
        yield
        scale = _rms_scale(h1)
        for j in range(N_HALF):
            cs = slice(j * DOT_COLS, (j + 1) * DOT_COLS)
            u2_ref[u2_w, :, cs] = (h1[j] * scale * nmlp_ref[:, cs] * (1.0 + sc_m[:, cs]) + sh_m[:, cs]).astype(BF16)

    stages = [project()] + [mix(r) for r in range(n_sub)] + [finish()]
    n_yields = [2] + [4] * n_sub + [1]
    slots = [(i, y) for i, n in enumerate(n_yields) for y in range(n)]
    schedule = {slot: [] for slot in slots}
    ups = [functools.partial(up_proj, j) for j in range(N_UP)]
    early = [s for s in slots if s[0] == 0 or (0 < s[0] <= n_sub and s[1] < 3)]
    for i, piece in enumerate(ups[:-1]):
        schedule[early[(i * len(early)) // (N_UP - 1)]].append(piece)
    schedule[slots[-1]].append(ups[-1])
    schedule[(1 + (n_sub - 1) // 2, 2)] += [functools.partial(down_proj, j) for j in range(1, N_HALF)] + [emit_output]
    down_proj(0)
    for i, stage in enumerate(stages):
        for y, _ in enumerate(stage):
            for piece in schedule[(i, y)]:
                piece()


def _const_spec(shape):
    zeros = (0,) * len(shape)
    return pl.BlockSpec(shape, lambda *_: zeros, pipeline_mode=pl.Buffered(1))


def kernel(x, c, w_ada, b_ada, norm_mix_w, w_in, lb_logits, g_norm_w, w_pool, pool_scale,
           w_out, norm_mlp_w, w_up, w_down, norm_final_w):
    B, T, D = x.shape
    assert D == D_MODEL and w_ada.shape[0] == 1, "single-layer kernel"
    tt = min(TIME_TILE, T)
    mr = min(MIX_ROWS, tt)
    n_chunks = mr // CHUNK
    assert T % tt == 0 and tt % mr == 0 and mr % CHUNK == 0 and mr >= W_MAX
    assert n_chunks & (n_chunks - 1) == 0 and n_chunks >= 4
    tiles_per_seq = T // tt
    n_tiles = B * tiles_per_seq

    n_mod_cols = N_MOD * D
    mod = pl.pallas_call(
        _mod_kernel,
        grid=(n_mod_cols // MOD_COL_BLOCK,),
        in_specs=[
            pl.BlockSpec((B, D), lambda j: (0, 0)),
            pl.BlockSpec((D, MOD_COL_BLOCK), lambda j: (0, j)),
            pl.BlockSpec((1, MOD_COL_BLOCK), lambda j: (0, j)),
        ],
        out_specs=pl.BlockSpec((B, MOD_COL_BLOCK), lambda j: (0, j)),
        out_shape=jax.ShapeDtypeStruct((B, n_mod_cols), F32),
        name="adaln_mod",
    )(c, w_ada[0], b_ada[0].reshape(1, n_mod_cols))
    mod = mod.reshape(B, N_MOD, D)

    def mix_tile(s):
        return jnp.minimum(s, n_tiles - 1)

    def down_tile(s):
        return jnp.maximum(s - 2, 0)

    row = lambda a: a.reshape(1, -1)
    out = pl.pallas_call(
        functools.partial(_layer_kernel, tt=tt, mr=mr, tiles_per_seq=tiles_per_seq, n_tiles=n_tiles),
        grid=(n_tiles + 2,),
        in_specs=[
            pl.BlockSpec((None, tt, D), lambda s: (mix_tile(s) // tiles_per_seq, mix_tile(s) % tiles_per_seq, 0)),
            pl.BlockSpec((None, N_MOD, D), lambda s: (mix_tile(s) // tiles_per_seq, 0, 0)),
            pl.BlockSpec((None, N_MOD, D), lambda s: (down_tile(s) // tiles_per_seq, 0, 0)),
            _const_spec((1, D)),
            _const_spec((D, D_IN)),
            _const_spec(lb_logits.shape),
            _const_spec((1, HEAD_DIM)),
            _const_spec((len(POOL_WINDOWS), POOL_GROUP_DIM, POOL_GROUP_DIM)),
            _const_spec((1, D_POOL)),
            _const_spec((D, D)),
            _const_spec((1, D)),
            _const_spec((D, D_FF)),
            _const_spec((D_FF, D)),
            _const_spec((1, D)),
        ],
        out_specs=pl.BlockSpec((None, tt, D),
                               lambda s: (down_tile(s) // tiles_per_seq, down_tile(s) % tiles_per_seq, 0)),
        out_shape=jax.ShapeDtypeStruct((B, T, D), x.dtype),
        scratch_shapes=[
            pltpu.VMEM((N_HEADS, HEAD_DIM, HEAD_DIM), F32),
            pltpu.VMEM((W_MAX + mr, D_POOL), F32),
            pltpu.VMEM((2, tt, D_FF), BF16),
            pltpu.VMEM((2, tt, D), F32),
            pltpu.VMEM((2, tt, D), BF16),
            pltpu.VMEM((tt, D_IN), F32),
            pltpu.VMEM((tt, D_MODEL), BF16),
        ],
        compiler_params=pltpu.CompilerParams(
            dimension_semantics=("arbitrary",),
            vmem_limit_bytes=VMEM_LIMIT_BYTES,
        ),
        name="hybrid_layer",
    )(x, mod, mod, row(norm_mix_w[0]), w_in[0].astype(BF16), lb_logits, row(g_norm_w[0]),
      w_pool[0].astype(BF16), row(pool_scale[0]), w_out[0].astype(BF16), row(norm_mlp_w[0]),
      w_up[0].astype(BF16), w_down[0].astype(BF16), row(norm_final_w))
    return out
```

```python
import functools

import jax
import jax.numpy as jnp
from jax import lax
from jax.experimental import pallas as pl
from jax.experimental.pallas import tpu as pltpu

F32 = jnp.float32
BF16 = jnp.bfloat16

D_MODEL = 1024
D_REC = 512
D_POOL = 512
HEAD_DIM = 128
N_HEADS = D_REC // HEAD_DIM
POOL_WINDOWS = (2, 4, 8, 16)
POOL_GROUP_DIM = D_POOL // len(POOL_WINDOWS)
W_MAX = max(POOL_WINDOWS)
D_IN = 4 * D_REC + D_POOL
D_FF = 4 * D_MODEL
N_MOD = 6
CHUNK = 32
EPS = 1e-6

TIME_TILE = 512
MIX_ROWS = 256
DOT_COLS = 512
N_IN = D_IN // DOT_COLS
N_UP = D_FF // DOT_COLS
N_HALF = D_MODEL // DOT_COLS
MOD_COL_BLOCK = 1024
VMEM_LIMIT_BYTES = 63 * 1024 * 1024 + 512 * 1024


def _dot(a, b):
    return jnp.dot(a, b, preferred_element_type=F32)


def _dot_nt(a, b):
    return lax.dot_general(a, b, (((1,), (1,)), ((), ())), preferred_element_type=F32)


def _sigmoid(x):
    return 1.0 / (1.0 + jnp.exp(-x))


def _rms_scale(parts):
    n = sum(a.shape[-1] for a in parts)
    ss = sum(jnp.sum(a * a, axis=-1, keepdims=True) for a in parts)
    return lax.rsqrt(ss * (1.0 / n) + EPS)


def _mod_kernel(c_ref, w_ref, b_ref, o_ref):
    c = c_ref[...]
    c_act = (c * _sigmoid(c)).astype(BF16)
    o_ref[...] = _dot(c_act, w_ref[...].astype(BF16)) + b_ref[...]


def _split2(x):
    hi = x.astype(BF16)
    lo = (x - hi.astype(F32)).astype(BF16)
    return hi, lo


def _fold_pool_kernel(wpool_ref, pscale_ref, wout_ref, o_ref):
    w = wpool_ref[...] * pscale_ref[...]
    o_ref[...] = jnp.dot(w, wout_ref[...], precision=lax.Precision.HIGHEST,
                         preferred_element_type=F32).astype(BF16)


def _scale_chunks(a3, rows):
    n_chunks, c, w = a3.shape
    out = [jnp.zeros((c, w), F32) if r is None else a3[n] * r for n, r in enumerate(rows)]
    return jnp.concatenate(out, axis=0).astype(BF16)


def _layer_kernel(*refs, **static):
    for parity in (0, 1):
        @pl.when(pl.program_id(0) % 2 == parity)
        def _():
            _layer_body(*refs, parity=parity, **static)


def _layer_body(x_ref, mod_mix_ref, mod_down_ref, nmix_ref, win_ref, lbl_ref, gnw_ref,
                wout_ref, nmlp_ref, wup_ref, wdown_ref, nfin_ref, o_ref,
                state_ref, pbuf_ref, hid_ref, h1_ref, u2_ref, proj_ref, mixin_ref,
                *, parity, tt, mr, tiles_per_seq, n_tiles):
    n_sub = tt // mr
    n_chunks = mr // CHUNK
    n_levels = n_chunks.bit_length() - 1
    step = pl.program_id(0)
    t_idx = jnp.minimum(step, n_tiles - 1) % tiles_per_seq
    h1_s = parity
    u2_w, u2_r = parity, 1 - parity
    hid_w, hid_r = 1 - parity, parity

    @pl.when(step == 0)
    def _():
        h1_ref[...] = jnp.zeros_like(h1_ref)
        u2_ref[...] = jnp.zeros_like(u2_ref)
        hid_ref[...] = jnp.zeros_like(hid_ref)

    @pl.when(t_idx == 0)
    def _():
        state_ref[...] = jnp.zeros_like(state_ref)
        pbuf_ref[0:W_MAX, :] = jnp.zeros((W_MAX, D_POOL), F32)

    mod = mod_mix_ref[...]
    sh_a, sc_a, gt_a, sh_m, sc_m = (mod[i:i + 1, :] for i in range(5))
    gt_m_down = mod_down_ref[5:6, :]

    u2_prev = u2_ref[u2_r]

    def up_proj(j):
        fs = slice(j * DOT_COLS, (j + 1) * DOT_COLS)
        a = jnp.maximum(_dot(u2_prev, wup_ref[:, fs]), 0.0)
        hid_ref[hid_w, :, fs] = (a * a).astype(BF16)

    hid_prev = hid_ref[hid_r]
    h2 = []

    def down_proj(j):
        cs = slice(j * DOT_COLS, (j + 1) * DOT_COLS)
        h2.append(h1_ref[h1_s, :, cs] + gt_m_down[:, cs] * _dot(hid_prev, wdown_ref[:, cs]))

    def emit_output():
        scale = _rms_scale(h2)
        for j in range(N_HALF):
            cs = slice(j * DOT_COLS, (j + 1) * DOT_COLS)
            o_ref[:, cs] = h2[j] * scale * nfin_ref[:, cs]

    def project():
        x = x_ref[...]
        u = (x * _rms_scale([x]) * nmix_ref[...] * (1.0 + sc_a) + sh_a).astype(BF16)
        for j in range(N_IN):
            cs = slice(j * DOT_COLS, (j + 1) * DOT_COLS)
            proj_ref[:, cs] = _dot(u, win_ref[:, cs])
            if j == 1:
                yield
        yield

    lbl = lbl_ref[...]
    lbe = jnp.exp(lbl - jnp.max(lbl, axis=0, keepdims=True))
    lb = lbe[0:1, :] / jnp.sum(lbe, axis=0, keepdims=True)
    row = lax.broadcasted_iota(jnp.int32, (mr, mr), 0)
    col = lax.broadcasted_iota(jnp.int32, (mr, mr), 1)
    row_c, col_c = row // CHUNK, col // CHUNK
    causal = (row_c == col_c) & (col <= row)
    tri = jnp.where(causal, 1.0, 0.0).astype(BF16)
    level_masks = [
        (row_c // (1 << l) == col_c // (1 << l)) & (row_c % (1 << l) >= (1 << l) // 2) & (col_c % (1 << l) < (1 << l) // 2)
        for l in range(1, n_levels)]
    ones = jnp.ones((1, D_REC), F32)

    def mix(r):
        rows = slice(r * mr, (r + 1) * mr)
        q = proj_ref[rows, 0 * D_REC:1 * D_REC]
        f_logit = proj_ref[rows, 1 * D_REC:2 * D_REC]
        forget = lb + (1.0 - lb) * _sigmoid(f_logit)
        k = 1.0 - forget
        logf = jnp.log(forget)
        qs = q * _sigmoid(q) * (HEAD_DIM ** -0.5)
        hi, lo = _split2(logf)
        b = _dot(tri, hi) + _dot(tri, lo)
        yield
        b3 = b.reshape(n_chunks, CHUNK, D_REC)
        b_mid = b3[:, CHUNK // 2 - 1:CHUNK // 2, :]
        b_last = b3[:, CHUNK - 1:CHUNK, :]
        qs3 = qs.reshape(n_chunks, CHUNK, D_REC)
        k3 = k.reshape(n_chunks, CHUNK, D_REC)
        q_d = (qs3 * jnp.exp(b3 - b_mid)).reshape(mr, D_REC).astype(BF16)
        k_d = (k3 * jnp.exp(b_mid - b3)).reshape(mr, D_REC).astype(BF16)
        q_in3 = qs3 * jnp.exp(b3)
        k_out3 = k3 * jnp.exp(b_last - b3)
        c = [jnp.zeros((1, D_REC), F32)]
        for n in range(n_chunks):
            c.append(c[n] + b_last[n])
        q_lv = [q_in3.reshape(mr, D_REC).astype(BF16)]
        k_lv = [k_out3.reshape(mr, D_REC).astype(BF16)]
        for l in range(2, n_levels + 1):
            size, half = 1 << l, (1 << l) // 2
            q_rows, k_rows = [], []
            for n in range(n_chunks):
                mid_b = (n // size) * size + half
                q_rows.append(jnp.exp(c[n] - c[mid_b]) if n % size >= half else None)
                k_rows.append(jnp.exp(c[mid_b] - c[n + 1]) if n % size < half else None)
            q_lv.append(_scale_chunks(q_in3, q_rows))
            k_lv.append(_scale_chunks(k_out3, k_rows))
        q_st = _scale_chunks(q_in3, [ones] + [jnp.exp(c[n]) for n in range(1, n_chunks)])
        k_st = _scale_chunks(k_out3, [jnp.exp(c[n_chunks] - c[n + 1]) for n in range(n_chunks - 1)] + [ones])
        sub_decay = jnp.exp(c[n_chunks])
        v = proj_ref[rows, 2 * D_REC:3 * D_REC]
        v_b = v.astype(BF16)
        o_heads = []
        for h in range(N_HEADS):
            hs = slice(h * HEAD_DIM, (h + 1) * HEAD_DIM)
            scores = _dot_nt(q_lv[-1][:, hs], k_lv[-1][:, hs])
            for l in range(n_levels - 2, -1, -1):
                scores = jnp.where(level_masks[l], _dot_nt(q_lv[l][:, hs], k_lv[l][:, hs]), scores)
            scores = jnp.where(causal, _dot_nt(q_d[:, hs], k_d[:, hs]), scores)
            st_t = state_ref[h]
            o_h = _dot(scores.astype(BF16), v_b[:, hs]) + _dot_nt(q_st[:, hs], st_t.astype(BF16))
            v_t = v[:, hs].T.astype(BF16)
            state_ref[h] = st_t * sub_decay[:, hs] + _dot(v_t, k_st[:, hs])
            g_h = proj_ref[rows, 3 * D_REC + h * HEAD_DIM:3 * D_REC + (h + 1) * HEAD_DIM]
            o_heads.append(o_h * _rms_scale([o_h]) * gnw_ref[...] * (g_h * _sigmoid(g_h)))
            if h % 2 == 1:
                yield
        p = proj_ref[rows, 4 * D_REC:4 * D_REC + D_POOL]
        pbuf_ref[W_MAX:W_MAX + mr, :] = p
        pos = t_idx * tt + r * mr + lax.broadcasted_iota(jnp.int32, (mr, POOL_GROUP_DIM), 0)
        for gi, w in enumerate(POOL_WINDOWS):
            cs = slice(gi * POOL_GROUP_DIM, (gi + 1) * POOL_GROUP_DIM)
            win_sum = pbuf_ref[:, cs]
            span = 1
            while span < w:
                win_sum = win_sum + pltpu.roll(win_sum, span, 0)
                span *= 2
            count = jnp.minimum(pos + 1, w).astype(F32)
            pooled = win_sum[W_MAX:, :] * (1.0 / count) - p[:, cs]
            o_heads.append(pooled)
        pbuf_ref[0:W_MAX, :] = pbuf_ref[mr:mr + W_MAX, :]
        mixin_ref[rows, :] = jnp.concatenate(o_heads, axis=1).astype(BF16)
        yield

    def finish():
        x = x_ref[...]
        mix_in = mixin_ref[...]
        h1 = []
        for j in range(N_HALF):
            cs = slice(j * DOT_COLS, (j + 1) * DOT_COLS)
            h1.append(x[:, cs] + gt_a[:, cs] * _dot(mix_in, wout_ref[:, cs]))
            h1_ref[h1_s, :, cs] = h1[j]
        yield
        scale = _rms_scale(h1)
        for j in range(N_HALF):
            cs = slice(j * DOT_COLS, (j + 1) * DOT_COLS)
            u2_ref[u2_w, :, cs] = (h1[j] * scale * nmlp_ref[:, cs] * (1.0 + sc_m[:, cs]) + sh_m[:, cs]).astype(BF16)

    stages = {"P": project(), "F": finish(), **{r: mix(r) for r in range(n_sub)}}
    if n_sub == 2:
        plan = ["D0", "P", "U0", "P", "U1", 0, "U2", 1, "U3", 0, "U4", 0, "D1", "E", 0, 1, "U5", 1, "U6", 1,
                "F", "U7", "F"]
    else:
        assert n_sub == 1
        plan = ["D0", "P", "U0", "P", 0, "U1", 0, "U2", 0, "U3", 0, "D1", "E", "U4", "U5", "U6", "F", "U7", "F"]
    for action in plan:
        if action in stages:
            next(stages[action], None)
        elif action == "E":
            emit_output()
        else:
            (up_proj if action[0] == "U" else down_proj)(int(action[1:]))


def _const_spec(shape):
    zeros = (0,) * len(shape)
    return pl.BlockSpec(shape, lambda *_: zeros, pipeline_mode=pl.Buffered(1))


def kernel(x, c, w_ada, b_ada, norm_mix_w, w_in, lb_logits, g_norm_w, w_pool, pool_scale,
           w_out, norm_mlp_w, w_up, w_down, norm_final_w):
    B, T, D = x.shape
    assert D == D_MODEL and w_ada.shape[0] == 1, "single-layer kernel"
    tt = min(TIME_TILE, T)
    mr = min(MIX_ROWS, tt)
    n_chunks = mr // CHUNK
    assert T % tt == 0 and tt % mr == 0 and mr % CHUNK == 0 and mr >= W_MAX
    assert n_chunks & (n_chunks - 1) == 0 and n_chunks >= 4
    tiles_per_seq = T // tt
    n_tiles = B * tiles_per_seq

    n_mod_cols = N_MOD * D
    mod = pl.pallas_call(
        _mod_kernel,
        grid=(n_mod_cols // MOD_COL_BLOCK,),
        in_specs=[
            pl.BlockSpec((B, D), lambda j: (0, 0)),
            pl.BlockSpec((D, MOD_COL_BLOCK), lambda j: (0, j)),
            pl.BlockSpec((1, MOD_COL_BLOCK), lambda j: (0, j)),
        ],
        out_specs=pl.BlockSpec((B, MOD_COL_BLOCK), lambda j: (0, j)),
        out_shape=jax.ShapeDtypeStruct((B, n_mod_cols), F32),
        name="adaln_mod",
    )(c, w_ada[0], b_ada[0].reshape(1, n_mod_cols))
    mod = mod.reshape(B, N_MOD, D)

    n_groups = len(POOL_WINDOWS)
    w_out_pool = pl.pallas_call(
        _fold_pool_kernel,
        grid=(n_groups,),
        in_specs=[
            pl.BlockSpec((None, POOL_GROUP_DIM, POOL_GROUP_DIM), lambda g: (g, 0, 0)),
            pl.BlockSpec((1, POOL_GROUP_DIM), lambda g: (0, g)),
            pl.BlockSpec((POOL_GROUP_DIM, D), lambda g: (g, 0)),
        ],
        out_specs=pl.BlockSpec((POOL_GROUP_DIM, D), lambda g: (g, 0)),
        out_shape=jax.ShapeDtypeStruct((D_POOL, D), BF16),
        name="fold_pool",
    )(w_pool[0], pool_scale[0].reshape(1, D_POOL), w_out[0][D_REC:, :])
    w_out_eff = jnp.concatenate([w_out[0][:D_REC, :].astype(BF16), w_out_pool], axis=0)

    def mix_tile(s):
        return jnp.minimum(s, n_tiles - 1)

    def down_tile(s):
        return jnp.maximum(s - 2, 0)

    row = lambda a: a.reshape(1, -1)
    out = pl.pallas_call(
        functools.partial(_layer_kernel, tt=tt, mr=mr, tiles_per_seq=tiles_per_seq, n_tiles=n_tiles),
        grid=(n_tiles + 2,),
        in_specs=[
            pl.BlockSpec((None, tt, D), lambda s: (mix_tile(s) // tiles_per_seq, mix_tile(s) % tiles_per_seq, 0)),
            pl.BlockSpec((None, N_MOD, D), lambda s: (mix_tile(s) // tiles_per_seq, 0, 0)),
            pl.BlockSpec((None, N_MOD, D), lambda s: (down_tile(s) // tiles_per_seq, 0, 0)),
            _const_spec((1, D)),
            _const_spec((D, D_IN)),
            _const_spec(lb_logits.shape),
            _const_spec((1, HEAD_DIM)),
            _const_spec((D, D)),
            _const_spec((1, D)),
            _const_spec((D, D_FF)),
            _const_spec((D_FF, D)),
            _const_spec((1, D)),
        ],
        out_specs=pl.BlockSpec((None, tt, D),
                               lambda s: (down_tile(s) // tiles_per_seq, down_tile(s) % tiles_per_seq, 0)),
        out_shape=jax.ShapeDtypeStruct((B, T, D), x.dtype),
        scratch_shapes=[
            pltpu.VMEM((N_HEADS, HEAD_DIM, HEAD_DIM), F32),
            pltpu.VMEM((W_MAX + mr, D_POOL), F32),
            pltpu.VMEM((2, tt, D_FF), BF16),
            pltpu.VMEM((2, tt, D), F32),
            pltpu.VMEM((2, tt, D), BF16),
            pltpu.VMEM((tt, D_IN), F32),
            pltpu.VMEM((tt, D_MODEL), BF16),
        ],
        compiler_params=pltpu.CompilerParams(
            dimension_semantics=("arbitrary",),
            vmem_limit_bytes=VMEM_LIMIT_BYTES,
        ),
        name="hybrid_layer",
    )(x, mod, mod, row(norm_mix_w[0]), w_in[0].astype(BF16), lb_logits, row(g_norm_w[0]),
      w_out_eff, row(norm_mlp_w[0]), w_up[0].astype(BF16), w_down[0].astype(BF16), row(norm_final_w))
    return out
```

```python
import functools

import jax
import jax.numpy as jnp
from jax import lax
from jax.experimental import pallas as pl
from jax.experimental.pallas import tpu as pltpu

F32 = jnp.float32
BF16 = jnp.bfloat16

D_MODEL = 1024
D_REC = 512
D_POOL = 512
HEAD_DIM = 128
N_HEADS = D_REC // HEAD_DIM
POOL_WINDOWS = (2, 4, 8, 16)
POOL_GROUP_DIM = D_POOL // len(POOL_WINDOWS)
W_MAX = max(POOL_WINDOWS)
D_IN = 4 * D_REC + D_POOL
D_FF = 4 * D_MODEL
N_MOD = 6
CHUNK = 32
EPS = 1e-6

TIME_TILE = 512
MIX_ROWS = 256
DOT_COLS = 512
N_IN = D_IN // DOT_COLS
UP_COLS = 1024
N_UP = D_FF // UP_COLS
N_HALF = D_MODEL // DOT_COLS
MOD_COL_BLOCK = 1024
VMEM_LIMIT_BYTES = 62 * 1024 * 1024


def _dot(a, b):
    return jnp.dot(a, b, preferred_element_type=F32)


def _dot_nt(a, b):
    return lax.dot_general(a, b, (((1,), (1,)), ((), ())), preferred_element_type=F32)


def _sigmoid(x):
    return 1.0 / (1.0 + jnp.exp(-x))


def _rms_scale(parts):
    n = sum(a.shape[-1] for a in parts)
    ss = sum(jnp.sum(a * a, axis=-1, keepdims=True) for a in parts)
    return lax.rsqrt(ss * (1.0 / n) + EPS)


def _mod_kernel(c_ref, w_ref, b_ref, o_ref):
    c = c_ref[...]
    c_act = (c * _sigmoid(c)).astype(BF16)
    o_ref[...] = _dot(c_act, w_ref[...].astype(BF16)) + b_ref[...]


def _split2(x):
    hi = x.astype(BF16)
    lo = (x - hi.astype(F32)).astype(BF16)
    return hi, lo


def _fold_pool_kernel(wpool_ref, pscale_ref, wout_ref, o_ref):
    w = wpool_ref[...] * pscale_ref[...]
    o_ref[...] = jnp.dot(w, wout_ref[...], precision=lax.Precision.HIGHEST,
                         preferred_element_type=F32).astype(BF16)


def _scale_chunks(a3, rows):
    n_chunks, c, w = a3.shape
    out = [jnp.zeros((c, w), F32) if r is None else a3[n] * r for n, r in enumerate(rows)]
    return jnp.concatenate(out, axis=0).astype(BF16)


def _layer_kernel(x_ref, mod_mix_ref, mod_down_ref, nmix_ref, win_ref, lbl_ref, gnw_ref,
                  wout_ref, nmlp_ref, wup_ref, wdown_ref, nfin_ref, o_ref,
                  state_ref, pbuf_ref, hid_ref, h1_ref, u2_ref, proj_ref, mixin_ref,
                  *, tt, mr, tiles_per_seq, n_tiles):
    n_sub = tt // mr
    n_chunks = mr // CHUNK
    n_levels = n_chunks.bit_length() - 1
    step = pl.program_id(0)
    t_idx = jnp.minimum(step, n_tiles - 1) % tiles_per_seq
    h1_s = step % 2
    u2_w, u2_r = step % 2, (step + 1) % 2
    hid_w, hid_r = (step + 1) % 2, step % 2

    @pl.when(step == 0)
    def _():
        h1_ref[...] = jnp.zeros_like(h1_ref)
        u2_ref[...] = jnp.zeros_like(u2_ref)
        hid_ref[...] = jnp.zeros_like(hid_ref)

    @pl.when(t_idx == 0)
    def _():
        state_ref[...] = jnp.zeros_like(state_ref)
        pbuf_ref[0:W_MAX, :] = jnp.zeros((W_MAX, D_POOL), F32)

    mod = mod_mix_ref[...]
    sh_a, sc_a, gt_a, sh_m, sc_m = (mod[i:i + 1, :] for i in range(5))
    gt_m_down = mod_down_ref[5:6, :]

    u2_prev = u2_ref[u2_r]

    def up_proj(j):
        fs = slice(j * UP_COLS, (j + 1) * UP_COLS)
        a = jnp.maximum(_dot(u2_prev, wup_ref[:, fs]), 0.0)
        hid_ref[hid_w, :, fs] = (a * a).astype(BF16)

    hid_prev = hid_ref[hid_r]
    h2 = []

    def down_proj(j):
        cs = slice(j * DOT_COLS, (j + 1) * DOT_COLS)
        h2.append(h1_ref[h1_s, :, cs] + gt_m_down[:, cs] * _dot(hid_prev, wdown_ref[:, cs]))

    def emit_output():
        scale = _rms_scale(h2)
        for j in range(N_HALF):
            cs = slice(j * DOT_COLS, (j + 1) * DOT_COLS)
            o_ref[:, cs] = h2[j] * scale * nfin_ref[:, cs]

    def project():
        x = x_ref[...]
        u = (x * _rms_scale([x]) * nmix_ref[...] * (1.0 + sc_a) + sh_a).astype(BF16)
        for j in range(N_IN):
            cs = slice(j * DOT_COLS, (j + 1) * DOT_COLS)
            proj_ref[:, cs] = _dot(u, win_ref[:, cs])
            if j == 1:
                yield
        yield

    lbl = lbl_ref[...]
    lbe = jnp.exp(lbl - jnp.max(lbl, axis=0, keepdims=True))
    lb = lbe[0:1, :] / jnp.sum(lbe, axis=0, keepdims=True)
    row = lax.broadcasted_iota(jnp.int32, (mr, mr), 0)
    col = lax.broadcasted_iota(jnp.int32, (mr, mr), 1)
    row_c, col_c = row // CHUNK, col // CHUNK
    causal = (row_c == col_c) & (col <= row)
    tri = jnp.where(causal, 1.0, 0.0).astype(BF16)
    level_masks = [
        (row_c // (1 << l) == col_c // (1 << l)) & (row_c % (1 << l) >= (1 << l) // 2) & (col_c % (1 << l) < (1 << l) // 2)
        for l in range(1, n_levels)]
    ones = jnp.ones((1, D_REC), F32)

    def mix(r):
        rows = slice(r * mr, (r + 1) * mr)
        q = proj_ref[rows, 0 * D_REC:1 * D_REC]
        f_logit = proj_ref[rows, 1 * D_REC:2 * D_REC]
        forget = lb + (1.0 - lb) * _sigmoid(f_logit)
        k = 1.0 - forget
        logf = jnp.log(forget)
        qs = q * _sigmoid(q) * (HEAD_DIM ** -0.5)
        hi, lo = _split2(logf)
        b = _dot(tri, hi) + _dot(tri, lo)
        yield
        b3 = b.reshape(n_chunks, CHUNK, D_REC)
        b_mid = b3[:, CHUNK // 2 - 1:CHUNK // 2, :]
        b_last = b3[:, CHUNK - 1:CHUNK, :]
        qs3 = qs.reshape(n_chunks, CHUNK, D_REC)
        k3 = k.reshape(n_chunks, CHUNK, D_REC)
        q_d = (qs3 * jnp.exp(b3 - b_mid)).reshape(mr, D_REC).astype(BF16)
        k_d = (k3 * jnp.exp(b_mid - b3)).reshape(mr, D_REC).astype(BF16)
        q_in3 = qs3 * jnp.exp(b3)
        k_out3 = k3 * jnp.exp(b_last - b3)
        c = [jnp.zeros((1, D_REC), F32)]
        for n in range(n_chunks):
            c.append(c[n] + b_last[n])
        q_lv = [q_in3.reshape(mr, D_REC).astype(BF16)]
        k_lv = [k_out3.reshape(mr, D_REC).astype(BF16)]
        for l in range(2, n_levels + 1):
            size, half = 1 << l, (1 << l) // 2
            q_rows, k_rows = [], []
            for n in range(n_chunks):
                mid_b = (n // size) * size + half
                q_rows.append(jnp.exp(c[n] - c[mid_b]) if n % size >= half else None)
                k_rows.append(jnp.exp(c[mid_b] - c[n + 1]) if n % size < half else None)
            q_lv.append(_scale_chunks(q_in3, q_rows))
            k_lv.append(_scale_chunks(k_out3, k_rows))
        q_st = _scale_chunks(q_in3, [ones] + [jnp.exp(c[n]) for n in range(1, n_chunks)])
        k_st = _scale_chunks(k_out3, [jnp.exp(c[n_chunks] - c[n + 1]) for n in range(n_chunks - 1)] + [ones])
        sub_decay = jnp.exp(c[n_chunks])
        v = proj_ref[rows, 2 * D_REC:3 * D_REC]
        v_b = v.astype(BF16)
        o_heads = []
        for h in range(N_HEADS):
            hs = slice(h * HEAD_DIM, (h + 1) * HEAD_DIM)
            scores = _dot_nt(q_lv[-1][:, hs], k_lv[-1][:, hs])
            for l in range(n_levels - 2, -1, -1):
                scores = jnp.where(level_masks[l], _dot_nt(q_lv[l][:, hs], k_lv[l][:, hs]), scores)
            scores = jnp.where(causal, _dot_nt(q_d[:, hs], k_d[:, hs]), scores)
            st_t = state_ref[h]
            o_h = _dot(scores.astype(BF16), v_b[:, hs]) + _dot_nt(q_st[:, hs], st_t.astype(BF16))
            v_t = v[:, hs].T.astype(BF16)
            state_ref[h] = st_t * sub_decay[:, hs] + _dot(v_t, k_st[:, hs])
            g_h = proj_ref[rows, 3 * D_REC + h * HEAD_DIM:3 * D_REC + (h + 1) * HEAD_DIM]
            o_heads.append(o_h * _rms_scale([o_h]) * gnw_ref[...] * (g_h * _sigmoid(g_h)))
            if h % 2 == 1:
                yield
        p = proj_ref[rows, 4 * D_REC:4 * D_REC + D_POOL]
        pbuf_ref[W_MAX:W_MAX + mr, :] = p
        pos = t_idx * tt + r * mr + lax.broadcasted_iota(jnp.int32, (mr, POOL_GROUP_DIM), 0)
        for gi, w in enumerate(POOL_WINDOWS):
            cs = slice(gi * POOL_GROUP_DIM, (gi + 1) * POOL_GROUP_DIM)
            win_sum = pbuf_ref[:, cs]
            span = 1
            while span < w:
                win_sum = win_sum + pltpu.roll(win_sum, span, 0)
                span *= 2
            count = jnp.minimum(pos + 1, w).astype(F32)
            pooled = win_sum[W_MAX:, :] * (1.0 / count) - p[:, cs]
            o_heads.append(pooled)
        pbuf_ref[0:W_MAX, :] = pbuf_ref[mr:mr + W_MAX, :]
        mixin_ref[rows, :] = jnp.concatenate(o_heads, axis=1).astype(BF16)
        yield

    def finish():
        x = x_ref[...]
        mix_in = mixin_ref[...]
        h1 = []
        for j in range(N_HALF):
            cs = slice(j * DOT_COLS, (j + 1) * DOT_COLS)
            h1.append(x[:, cs] + gt_a[:, cs] * _dot(mix_in, wout_ref[:, cs]))
            h1_ref[h1_s, :, cs] = h1[j]
        yield
        scale = _rms_scale(h1)
        for j in range(N_HALF):
            cs = slice(j * DOT_COLS, (j + 1) * DOT_COLS)
            u2_ref[u2_w, :, cs] = (h1[j] * scale * nmlp_ref[:, cs] * (1.0 + sc_m[:, cs]) + sh_m[:, cs]).astype(BF16)

    stages = {"P": project(), "F": finish(), **{r: mix(r) for r in range(n_sub)}}
    if n_sub == 2:
        plan = ["D0", "P", "U0", "P", 0, "U1", 1, 0, "U2", 0, "D1", "E", 0, 1, 1, "U3", 1, "F", "F"]
    else:
        assert n_sub == 1 and N_UP == 4
        plan = ["D0", "P", "U0", "P", 0, "U1", 0, "U2", 0, "D1", "E", 0, "U3", "F", "F"]
    for action in plan:
        if action in stages:
            next(stages[action], None)
        elif action == "E":
            emit_output()
        else:
            (up_proj if action[0] == "U" else down_proj)(int(action[1:]))


def _const_spec(shape):
    zeros = (0,) * len(shape)
    return pl.BlockSpec(shape, lambda *_: zeros, pipeline_mode=pl.Buffered(1))


def kernel(x, c, w_ada, b_ada, norm_mix_w, w_in, lb_logits, g_norm_w, w_pool, pool_scale,
           w_out, norm_mlp_w, w_up, w_down, norm_final_w):
    B, T, D = x.shape
    assert D == D_MODEL and w_ada.shape[0] == 1, "single-layer kernel"
    tt = min(TIME_TILE, T)
    mr = min(MIX_ROWS, tt)
    n_chunks = mr // CHUNK
    assert T % tt == 0 and tt % mr == 0 and mr % CHUNK == 0 and mr >= W_MAX
    assert n_chunks & (n_chunks - 1) == 0 and n_chunks >= 4
    tiles_per_seq = T // tt
    n_tiles = B * tiles_per_seq

    n_mod_cols = N_MOD * D
    mod = pl.pallas_call(
        _mod_kernel,
        grid=(n_mod_cols // MOD_COL_BLOCK,),
        in_specs=[
            pl.BlockSpec((B, D), lambda j: (0, 0)),
            pl.BlockSpec((D, MOD_COL_BLOCK), lambda j: (0, j)),
            pl.BlockSpec((1, MOD_COL_BLOCK), lambda j: (0, j)),
        ],
        out_specs=pl.BlockSpec((B, MOD_COL_BLOCK), lambda j: (0, j)),
        out_shape=jax.ShapeDtypeStruct((B, n_mod_cols), F32),
        name="adaln_mod",
    )(c, w_ada[0], b_ada[0].reshape(1, n_mod_cols))
    mod = mod.reshape(B, N_MOD, D)

    n_groups = len(POOL_WINDOWS)
    w_out_pool = pl.pallas_call(
        _fold_pool_kernel,
        grid=(n_groups,),
        in_specs=[
            pl.BlockSpec((None, POOL_GROUP_DIM, POOL_GROUP_DIM), lambda g: (g, 0, 0)),
            pl.BlockSpec((1, POOL_GROUP_DIM), lambda g: (0, g)),
            pl.BlockSpec((POOL_GROUP_DIM, D), lambda g: (g, 0)),
        ],
        out_specs=pl.BlockSpec((POOL_GROUP_DIM, D), lambda g: (g, 0)),
        out_shape=jax.ShapeDtypeStruct((D_POOL, D), BF16),
        name="fold_pool",
    )(w_pool[0], pool_scale[0].reshape(1, D_POOL), w_out[0][D_REC:, :])
    w_out_eff = jnp.concatenate([w_out[0][:D_REC, :].astype(BF16), w_out_pool], axis=0)

    def mix_tile(s):
        return jnp.minimum(s, n_tiles - 1)

    def down_tile(s):
        return jnp.maximum(s - 2, 0)

    row = lambda a: a.reshape(1, -1)
    out = pl.pallas_call(
        functools.partial(_layer_kernel, tt=tt, mr=mr, tiles_per_seq=tiles_per_seq, n_tiles=n_tiles),
        grid=(n_tiles + 2,),
        in_specs=[
            pl.BlockSpec((None, tt, D), lambda s: (mix_tile(s) // tiles_per_seq, mix_tile(s) % tiles_per_seq, 0)),
            pl.BlockSpec((None, N_MOD, D), lambda s: (mix_tile(s) // tiles_per_seq, 0, 0)),
            pl.BlockSpec((None, N_MOD, D), lambda s: (down_tile(s) // tiles_per_seq, 0, 0)),
            _const_spec((1, D)),
            _const_spec((D, D_IN)),
            _const_spec(lb_logits.shape),
            _const_spec((1, HEAD_DIM)),
            _const_spec((D, D)),
            _const_spec((1, D)),
            _const_spec((D, D_FF)),
            _const_spec((D_FF, D)),
            _const_spec((1, D)),
        ],
        out_specs=pl.BlockSpec((None, tt, D),
                               lambda s: (down_tile(s) // tiles_per_seq, down_tile(s) % tiles_per_seq, 0)),
        out_shape=jax.ShapeDtypeStruct((B, T, D), x.dtype),
        scratch_shapes=[
            pltpu.VMEM((N_HEADS, HEAD_DIM, HEAD_DIM), F32),
            pltpu.VMEM((W_MAX + mr, D_POOL), F32),
            pltpu.VMEM((2, tt, D_FF), BF16),
            pltpu.VMEM((2, tt, D), F32),
            pltpu.VMEM((2, tt, D), BF16),
            pltpu.VMEM((tt, D_IN), F32),
            pltpu.VMEM((tt, D_MODEL), BF16),
        ],
        compiler_params=pltpu.CompilerParams(
            dimension_semantics=("arbitrary",),
            vmem_limit_bytes=VMEM_LIMIT_BYTES,
        ),
        name="hybrid_layer",
    )(x, mod, mod, row(norm_mix_w[0]), w_in[0].astype(BF16), lb_logits, row(g_norm_w[0]),
      w_out_eff, row(norm_mlp_w[0]), w_up[0].astype(BF16), w_down[0].astype(BF16), row(norm_final_w))
    return out
```

```python
import functools

import jax
import jax.numpy as jnp
from jax import lax
from jax.experimental import pallas as pl
from jax.experimental.pallas import tpu as pltpu

F32 = jnp.float32
BF16 = jnp.bfloat16

D_MODEL = 1024
D_REC = 512
D_POOL = 512
HEAD_DIM = 128
N_HEADS = D_REC // HEAD_DIM
POOL_WINDOWS = (2, 4, 8, 16)
POOL_GROUP_DIM = D_POOL // len(POOL_WINDOWS)
W_MAX = max(POOL_WINDOWS)
D_IN = 4 * D_REC + D_POOL
D_FF = 4 * D_MODEL
N_MOD = 6
CHUNK = 32
EPS = 1e-6

TIME_TILE = 512
MIX_ROWS = 256
DOT_COLS = 512
N_IN = D_IN // DOT_COLS
UP_COLS = 1024
N_UP = D_FF // UP_COLS
N_HALF = D_MODEL // DOT_COLS
MOD_COL_BLOCK = 1024
VMEM_LIMIT_BYTES = 63 * 1024 * 1024 + 512 * 1024


def _dot(a, b):
    return jnp.dot(a, b, preferred_element_type=F32)


def _dot_nt(a, b):
    return lax.dot_general(a, b, (((1,), (1,)), ((), ())), preferred_element_type=F32)


def _sigmoid(x):
    return 1.0 / (1.0 + jnp.exp(-x))


def _rms_scale(parts):
    n = sum(a.shape[-1] for a in parts)
    ss = sum(jnp.sum(a * a, axis=-1, keepdims=True) for a in parts)
    return lax.rsqrt(ss * (1.0 / n) + EPS)


def _mod_kernel(c_ref, w_ref, b_ref, o_ref):
    c = c_ref[...]
    c_act = (c * _sigmoid(c)).astype(BF16)
    o_ref[...] = _dot(c_act, w_ref[...].astype(BF16)) + b_ref[...]


def _split2(x):
    hi = x.astype(BF16)
    lo = (x - hi.astype(F32)).astype(BF16)
    return hi, lo


def _fold_pool_kernel(wpool_ref, pscale_ref, wout_ref, o_ref):
    w = wpool_ref[...] * pscale_ref[...]
    o_ref[...] = jnp.dot(w, wout_ref[...], precision=lax.Precision.HIGHEST,
                         preferred_element_type=F32).astype(BF16)


def _scale_chunks(a3, rows):
    n_chunks, c, w = a3.shape
    out = [jnp.zeros((c, w), F32) if r is None else a3[n] * r for n, r in enumerate(rows)]
    return jnp.concatenate(out, axis=0).astype(BF16)


def _layer_kernel(*refs, n_tiles, **static):
    step = pl.program_id(0)
    pl.when(step < n_tiles)(functools.partial(_layer_body, *refs, do_mix=True, n_tiles=n_tiles, **static))
    pl.when(step >= n_tiles)(functools.partial(_layer_body, *refs, do_mix=False, n_tiles=n_tiles, **static))


def _layer_body(x_ref, mod_mix_ref, mod_down_ref, nmix_ref, win_ref, lbl_ref, gnw_ref,
                wout_ref, nmlp_ref, wup_ref, wdown_ref, nfin_ref, o_ref,
                state_ref, pbuf_ref, hid_ref, h1_ref, u2_ref, proj_ref, mixin_ref,
                *, do_mix, tt, mr, tiles_per_seq, n_tiles):
    n_sub = tt // mr
    n_chunks = mr // CHUNK
    n_levels = n_chunks.bit_length() - 1
    step = pl.program_id(0)
    t_idx = jnp.minimum(step, n_tiles - 1) % tiles_per_seq
    h1_s = step % 2
    u2_w, u2_r = step % 2, (step + 1) % 2
    hid_w, hid_r = (step + 1) % 2, step % 2

    if do_mix:
        @pl.when(step == 0)
        def _():
            h1_ref[...] = jnp.zeros_like(h1_ref)
            u2_ref[...] = jnp.zeros_like(u2_ref)
            hid_ref[...] = jnp.zeros_like(hid_ref)

        @pl.when(t_idx == 0)
        def _():
            state_ref[...] = jnp.zeros_like(state_ref)
            pbuf_ref[0:W_MAX, :] = jnp.zeros((W_MAX, D_POOL), F32)

    mod = mod_mix_ref[...]
    sh_a, sc_a, gt_a, sh_m, sc_m = (mod[i:i + 1, :] for i in range(5))
    gt_m_down = mod_down_ref[5:6, :]

    u2_prev = u2_ref[u2_r]

    def up_proj(j):
        fs = slice(j * UP_COLS, (j + 1) * UP_COLS)
        a = jnp.maximum(_dot(u2_prev, wup_ref[:, fs]), 0.0)
        hid_ref[hid_w, :, fs] = (a * a).astype(BF16)

    hid_prev = hid_ref[hid_r]
    h2 = []

    def down_proj(j):
        cs = slice(j * DOT_COLS, (j + 1) * DOT_COLS)
        h2.append(h1_ref[h1_s, :, cs] + gt_m_down[:, cs] * _dot(hid_prev, wdown_ref[:, cs]))

    def emit_output():
        scale = _rms_scale(h2)
        for j in range(N_HALF):
            cs = slice(j * DOT_COLS, (j + 1) * DOT_COLS)
            o_ref[:, cs] = h2[j] * scale * nfin_ref[:, cs]

    def project():
        x = x_ref[...]
        u = (x * _rms_scale([x]) * nmix_ref[...] * (1.0 + sc_a) + sh_a).astype(BF16)
        for j in range(N_IN):
            cs = slice(j * DOT_COLS, (j + 1) * DOT_COLS)
            proj_ref[:, cs] = _dot(u, win_ref[:, cs])
            if j == 1:
                yield
        yield

    lbl = lbl_ref[...]
    lbe = jnp.exp(lbl - jnp.max(lbl, axis=0, keepdims=True))
    lb = lbe[0:1, :] / jnp.sum(lbe, axis=0, keepdims=True)
    row = lax.broadcasted_iota(jnp.int32, (mr, mr), 0)
    col = lax.broadcasted_iota(jnp.int32, (mr, mr), 1)
    row_c, col_c = row // CHUNK, col // CHUNK
    causal = (row_c == col_c) & (col <= row)
    tri = jnp.where(causal, 1.0, 0.0).astype(BF16)
    level_masks = [
        (row_c // (1 << l) == col_c // (1 << l)) & (row_c % (1 << l) >= (1 << l) // 2) & (col_c % (1 << l) < (1 << l) // 2)
        for l in range(1, n_levels)]
    ones = jnp.ones((1, D_REC), F32)

    def mix(r):
        rows = slice(r * mr, (r + 1) * mr)
        q = proj_ref[rows, 0 * D_REC:1 * D_REC]
        f_logit = proj_ref[rows, 1 * D_REC:2 * D_REC]
        forget = lb + (1.0 - lb) * _sigmoid(f_logit)
        k = 1.0 - forget
        logf = jnp.log(forget)
        qs = q * _sigmoid(q) * (HEAD_DIM ** -0.5)
        hi, lo = _split2(logf)
        b = _dot(tri, hi) + _dot(tri, lo)
        yield
        b3 = b.reshape(n_chunks, CHUNK, D_REC)
        b_mid = b3[:, CHUNK // 2 - 1:CHUNK // 2, :]
        b_last = b3[:, CHUNK - 1:CHUNK, :]
        qs3 = qs.reshape(n_chunks, CHUNK, D_REC)
        k3 = k.reshape(n_chunks, CHUNK, D_REC)
        q_d = (qs3 * jnp.exp(b3 - b_mid)).reshape(mr, D_REC).astype(BF16)
        k_d = (k3 * jnp.exp(b_mid - b3)).reshape(mr, D_REC).astype(BF16)
        q_in3 = qs3 * jnp.exp(b3)
        k_out3 = k3 * jnp.exp(b_last - b3)
        c = [jnp.zeros((1, D_REC), F32)]
        for n in range(n_chunks):
            c.append(c[n] + b_last[n])
        q_lv = [q_in3.reshape(mr, D_REC).astype(BF16)]
        k_lv = [k_out3.reshape(mr, D_REC).astype(BF16)]
        for l in range(2, n_levels + 1):
            size, half = 1 << l, (1 << l) // 2
            q_rows, k_rows = [], []
            for n in range(n_chunks):
                mid_b = (n // size) * size + half
                q_rows.append(jnp.exp(c[n] - c[mid_b]) if n % size >= half else None)
                k_rows.append(jnp.exp(c[mid_b] - c[n + 1]) if n % size < half else None)
            q_lv.append(_scale_chunks(q_in3, q_rows))
            k_lv.append(_scale_chunks(k_out3, k_rows))
        q_st = _scale_chunks(q_in3, [ones] + [jnp.exp(c[n]) for n in range(1, n_chunks)])
        k_st = _scale_chunks(k_out3, [jnp.exp(c[n_chunks] - c[n + 1]) for n in range(n_chunks - 1)] + [ones])
        sub_decay = jnp.exp(c[n_chunks])
        v = proj_ref[rows, 2 * D_REC:3 * D_REC]
        v_b = v.astype(BF16)
        o_heads = []
        for h in range(N_HEADS):
            hs = slice(h * HEAD_DIM, (h + 1) * HEAD_DIM)
            scores = _dot_nt(q_lv[-1][:, hs], k_lv[-1][:, hs])
            for l in range(n_levels - 2, -1, -1):
                scores = jnp.where(level_masks[l], _dot_nt(q_lv[l][:, hs], k_lv[l][:, hs]), scores)
            scores = jnp.where(causal, _dot_nt(q_d[:, hs], k_d[:, hs]), scores)
            st_t = state_ref[h]
            o_h = _dot(scores.astype(BF16), v_b[:, hs]) + _dot_nt(q_st[:, hs], st_t.astype(BF16))
            v_t = v[:, hs].T.astype(BF16)
            state_ref[h] = st_t * sub_decay[:, hs] + _dot(v_t, k_st[:, hs])
            g_h = proj_ref[rows, 3 * D_REC + h * HEAD_DIM:3 * D_REC + (h + 1) * HEAD_DIM]
            o_heads.append(o_h * _rms_scale([o_h]) * gnw_ref[...] * (g_h * _sigmoid(g_h)))
            if h % 2 == 1:
                yield
        p = proj_ref[rows, 4 * D_REC:4 * D_REC + D_POOL]
        pbuf_ref[W_MAX:W_MAX + mr, :] = p
        pos = t_idx * tt + r * mr + lax.broadcasted_iota(jnp.int32, (mr, POOL_GROUP_DIM), 0)
        for gi, w in enumerate(POOL_WINDOWS):
            cs = slice(gi * POOL_GROUP_DIM, (gi + 1) * POOL_GROUP_DIM)
            win_sum = pbuf_ref[:, cs]
            span = 1
            while span < w:
                win_sum = win_sum + pltpu.roll(win_sum, span, 0)
                span *= 2
            count = jnp.minimum(pos + 1, w).astype(F32)
            pooled = win_sum[W_MAX:, :] * (1.0 / count) - p[:, cs]
            o_heads.append(pooled)
        pbuf_ref[0:W_MAX, :] = pbuf_ref[mr:mr + W_MAX, :]
        mixin_ref[rows, :] = jnp.concatenate(o_heads, axis=1).astype(BF16)
        yield

    def finish():
        x = x_ref[...]
        mix_in = mixin_ref[...]
        h1 = []
        for j in range(N_HALF):
            cs = slice(j * DOT_COLS, (j + 1) * DOT_COLS)
            h1.append(x[:, cs] + gt_a[:, cs] * _dot(mix_in, wout_ref[:, cs]))
            h1_ref[h1_s, :, cs] = h1[j]
        yield
        scale = _rms_scale(h1)
        for j in range(N_HALF):
            cs = slice(j * DOT_COLS, (j + 1) * DOT_COLS)
            u2_ref[u2_w, :, cs] = (h1[j] * scale * nmlp_ref[:, cs] * (1.0 + sc_m[:, cs]) + sh_m[:, cs]).astype(BF16)

    stages = {"P": project(), "F": finish(), **{r: mix(r) for r in range(n_sub)}}
    if n_sub == 2:
        plan = ["D0", "P", "U0", "P", 0, "U1", 1, 0, "U2", 0, "D1", "E", 0, 1, 1, "U3", 1, "F", "F"]
    else:
        assert n_sub == 1 and N_UP == 4
        plan = ["D0", "P", "U0", "P", 0, "U1", 0, "U2", 0, "D1", "E", 0, "U3", "F", "F"]
    for action in plan:
        if action in stages:
            if do_mix:
                next(stages[action], None)
        elif action == "E":
            emit_output()
        else:
            (up_proj if action[0] == "U" else down_proj)(int(action[1:]))


def _const_spec(shape):
    zeros = (0,) * len(shape)
    return pl.BlockSpec(shape, lambda *_: zeros, pipeline_mode=pl.Buffered(1))


def kernel(x, c, w_ada, b_ada, norm_mix_w, w_in, lb_logits, g_norm_w, w_pool, pool_scale,
           w_out, norm_mlp_w, w_up, w_down, norm_final_w):
    B, T, D = x.shape
    assert D == D_MODEL and w_ada.shape[0] == 1, "single-layer kernel"
    tt = min(TIME_TILE, T)
    mr = min(MIX_ROWS, tt)
    n_chunks = mr // CHUNK
    assert T % tt == 0 and tt % mr == 0 and mr % CHUNK == 0 and mr >= W_MAX
    assert n_chunks & (n_chunks - 1) == 0 and n_chunks >= 4
    tiles_per_seq = T // tt
    n_tiles = B * tiles_per_seq

    n_mod_cols = N_MOD * D
    mod = pl.pallas_call(
        _mod_kernel,
        grid=(n_mod_cols // MOD_COL_BLOCK,),
        in_specs=[
            pl.BlockSpec((B, D), lambda j: (0, 0)),
            pl.BlockSpec((D, MOD_COL_BLOCK), lambda j: (0, j)),
            pl.BlockSpec((1, MOD_COL_BLOCK), lambda j: (0, j)),
        ],
        out_specs=pl.BlockSpec((B, MOD_COL_BLOCK), lambda j: (0, j)),
        out_shape=jax.ShapeDtypeStruct((B, n_mod_cols), F32),
        name="adaln_mod",
    )(c, w_ada[0], b_ada[0].reshape(1, n_mod_cols))
    mod = mod.reshape(B, N_MOD, D)

    n_groups = len(POOL_WINDOWS)
    w_out_pool = pl.pallas_call(
        _fold_pool_kernel,
        grid=(n_groups,),
        in_specs=[
            pl.BlockSpec((None, POOL_GROUP_DIM, POOL_GROUP_DIM), lambda g: (g, 0, 0)),
            pl.BlockSpec((1, POOL_GROUP_DIM), lambda g: (0, g)),
            pl.BlockSpec((POOL_GROUP_DIM, D), lambda g: (g, 0)),
        ],
        out_specs=pl.BlockSpec((POOL_GROUP_DIM, D), lambda g: (g, 0)),
        out_shape=jax.ShapeDtypeStruct((D_POOL, D), BF16),
        name="fold_pool",
    )(w_pool[0], pool_scale[0].reshape(1, D_POOL), w_out[0][D_REC:, :])
    w_out_eff = jnp.concatenate([w_out[0][:D_REC, :].astype(BF16), w_out_pool], axis=0)

    def mix_tile(s):
        return jnp.minimum(s, n_tiles - 1)

    def down_tile(s):
        return jnp.maximum(s - 2, 0)

    row = lambda a: a.reshape(1, -1)
    out = pl.pallas_call(
        functools.partial(_layer_kernel, tt=tt, mr=mr, tiles_per_seq=tiles_per_seq, n_tiles=n_tiles),
        grid=(n_tiles + 2,),
        in_specs=[
            pl.BlockSpec((None, tt, D), lambda s: (mix_tile(s) // tiles_per_seq, mix_tile(s) % tiles_per_seq, 0)),
            pl.BlockSpec((None, N_MOD, D), lambda s: (mix_tile(s) // tiles_per_seq, 0, 0)),
            pl.BlockSpec((None, N_MOD, D), lambda s: (down_tile(s) // tiles_per_seq, 0, 0)),
            _const_spec((1, D)),
            _const_spec((D, D_IN)),
            _const_spec(lb_logits.shape),
            _const_spec((1, HEAD_DIM)),
            _const_spec((D, D)),
            _const_spec((1, D)),
            _const_spec((D, D_FF)),
            _const_spec((D_FF, D)),
            _const_spec((1, D)),
        ],
        out_specs=pl.BlockSpec((None, tt, D),
                               lambda s: (down_tile(s) // tiles_per_seq, down_tile(s) % tiles_per_seq, 0)),
        out_shape=jax.ShapeDtypeStruct((B, T, D), x.dtype),
        scratch_shapes=[
            pltpu.VMEM((N_HEADS, HEAD_DIM, HEAD_DIM), F32),
            pltpu.VMEM((W_MAX + mr, D_POOL), F32),
            pltpu.VMEM((2, tt, D_FF), BF16),
            pltpu.VMEM((2, tt, D), F32),
            pltpu.VMEM((2, tt, D), BF16),
            pltpu.VMEM((tt, D_IN), F32),
            pltpu.VMEM((tt, D_MODEL), BF16),
        ],
        compiler_params=pltpu.CompilerParams(
            dimension_semantics=("arbitrary",),
            vmem_limit_bytes=VMEM_LIMIT_BYTES,
        ),
        name="hybrid_layer",
    )(x, mod, mod, row(norm_mix_w[0]), w_in[0].astype(BF16), lb_logits, row(g_norm_w[0]),
      w_out_eff, row(norm_mlp_w[0]), w_up[0].astype(BF16), w_down[0].astype(BF16), row(norm_final_w))
    return out
```

```python
import functools

import jax
import jax.numpy as jnp
from jax import lax
from jax.experimental import pallas as pl
from jax.experimental.pallas import tpu as pltpu

F32 = jnp.float32
BF16 = jnp.bfloat16

D_MODEL = 1024
D_REC = 512
D_POOL = 512
HEAD_DIM = 128
N_HEADS = D_REC // HEAD_DIM
POOL_WINDOWS = (2, 4, 8, 16)
POOL_GROUP_DIM = D_POOL // len(POOL_WINDOWS)
W_MAX = max(POOL_WINDOWS)
D_IN = 4 * D_REC + D_POOL
D_FF = 4 * D_MODEL
N_MOD = 6
CHUNK = 32
EPS = 1e-6

TIME_TILE = 512
MIX_ROWS = 256
DOT_COLS = 512
N_IN = D_IN // DOT_COLS
UP_COLS = 1024
N_UP = D_FF // UP_COLS
N_HALF = D_MODEL // DOT_COLS
MOD_COL_BLOCK = 1024
VMEM_LIMIT_BYTES = 63 * 1024 * 1024 + 512 * 1024


def _dot(a, b):
    return jnp.dot(a, b, preferred_element_type=F32)


def _dot_nt(a, b):
    return lax.dot_general(a, b, (((1,), (1,)), ((), ())), preferred_element_type=F32)


def _sigmoid(x):
    return 1.0 / (1.0 + jnp.exp(-x))


def _rms_scale(parts):
    n = sum(a.shape[-1] for a in parts)
    ss = sum(jnp.sum(a * a, axis=-1, keepdims=True) for a in parts)
    return lax.rsqrt(ss * (1.0 / n) + EPS)


def _mod_kernel(c_ref, w_ref, b_ref, o_ref):
    c = c_ref[...]
    c_act = (c * _sigmoid(c)).astype(BF16)
    o_ref[...] = _dot(c_act, w_ref[...].astype(BF16)) + b_ref[...]


def _split2(x):
    hi = x.astype(BF16)
    lo = (x - hi.astype(F32)).astype(BF16)
    return hi, lo


def _fold_pool_kernel(wpool_ref, pscale_ref, wout_ref, o_ref):
    w = wpool_ref[...] * pscale_ref[...]
    o_ref[...] = jnp.dot(w, wout_ref[...], precision=lax.Precision.HIGHEST,
                         preferred_element_type=F32).astype(BF16)


def _scale_chunks(a3, rows):
    n_chunks, c, w = a3.shape
    out = [jnp.zeros((c, w), F32) if r is None else a3[n] * r for n, r in enumerate(rows)]
    return jnp.concatenate(out, axis=0).astype(BF16)


def _layer_kernel(*refs, n_tiles, **static):
    step = pl.program_id(0)
    pl.when(step < n_tiles)(functools.partial(_layer_body, *refs, do_mix=True, n_tiles=n_tiles, **static))
    pl.when(step >= n_tiles)(functools.partial(_layer_body, *refs, do_mix=False, n_tiles=n_tiles, **static))


def _layer_body(x_ref, mod_mix_ref, mod_up_ref, mod_down_ref, nmix_ref, win_ref, lbl_ref, gnw_ref,
                wout_ref, nmlp_ref, wup_ref, wdown_ref, nfin_ref, o_ref,
                state_ref, pbuf_ref, hid_ref, h1_ref, proj_ref, mixin_ref,
                *, do_mix, tt, mr, tiles_per_seq, n_tiles):
    n_sub = tt // mr
    n_chunks = mr // CHUNK
    n_levels = n_chunks.bit_length() - 1
    step = pl.program_id(0)
    t_idx = jnp.minimum(step, n_tiles - 1) % tiles_per_seq
    h1_s, h1_up = step % 2, (step + 1) % 2
    hid_w, hid_r = (step + 1) % 2, step % 2

    if do_mix:
        @pl.when(step == 0)
        def _():
            h1_ref[...] = jnp.zeros_like(h1_ref)
            hid_ref[...] = jnp.zeros_like(hid_ref)

        @pl.when(t_idx == 0)
        def _():
            state_ref[...] = jnp.zeros_like(state_ref)
            pbuf_ref[0:W_MAX, :] = jnp.zeros((W_MAX, D_POOL), F32)

    mod = mod_mix_ref[...]
    sh_a, sc_a, gt_a = (mod[i:i + 1, :] for i in range(3))
    sh_m, sc_m = mod_up_ref[3:4, :], mod_up_ref[4:5, :]
    gt_m_down = mod_down_ref[5:6, :]

    h1_up_tile = h1_ref[h1_up]
    u2_prev = (h1_up_tile * _rms_scale([h1_up_tile]) * nmlp_ref[...] * (1.0 + sc_m) + sh_m).astype(BF16)

    def up_proj(j):
        fs = slice(j * UP_COLS, (j + 1) * UP_COLS)
        a = jnp.maximum(_dot(u2_prev, wup_ref[:, fs]), 0.0)
        hid_ref[hid_w, :, fs] = (a * a).astype(BF16)

    hid_prev = hid_ref[hid_r]
    h2 = []

    def down_proj(j):
        cs = slice(j * DOT_COLS, (j + 1) * DOT_COLS)
        h2.append(h1_ref[h1_s, :, cs] + gt_m_down[:, cs] * _dot(hid_prev, wdown_ref[:, cs]))

    def emit_output():
        scale = _rms_scale(h2)
        for j in range(N_HALF):
            cs = slice(j * DOT_COLS, (j + 1) * DOT_COLS)
            o_ref[:, cs] = h2[j] * scale * nfin_ref[:, cs]

    def project():
        x = x_ref[...]
        u = (x * _rms_scale([x]) * nmix_ref[...] * (1.0 + sc_a) + sh_a).astype(BF16)
        for j in range(N_IN):
            cs = slice(j * DOT_COLS, (j + 1) * DOT_COLS)
            proj_ref[:, cs] = _dot(u, win_ref[:, cs])
            if j == 1:
                yield
        yield

    lbl = lbl_ref[...]
    lbe = jnp.exp(lbl - jnp.max(lbl, axis=0, keepdims=True))
    lb = lbe[0:1, :] / jnp.sum(lbe, axis=0, keepdims=True)
    row = lax.broadcasted_iota(jnp.int32, (mr, mr), 0)
    col = lax.broadcasted_iota(jnp.int32, (mr, mr), 1)
    row_c, col_c = row // CHUNK, col // CHUNK
    causal = (row_c == col_c) & (col <= row)
    tri = jnp.where(causal, 1.0, 0.0).astype(BF16)
    level_masks = [
        (row_c // (1 << l) == col_c // (1 << l)) & (row_c % (1 << l) >= (1 << l) // 2) & (col_c % (1 << l) < (1 << l) // 2)
        for l in range(1, n_levels)]
    ones = jnp.ones((1, D_REC), F32)

    def mix(r):
        rows = slice(r * mr, (r + 1) * mr)
        q = proj_ref[rows, 0 * D_REC:1 * D_REC]
        f_logit = proj_ref[rows, 1 * D_REC:2 * D_REC]
        forget = lb + (1.0 - lb) * _sigmoid(f_logit)
        k = 1.0 - forget
        logf = jnp.log(forget)
        qs = q * _sigmoid(q) * (HEAD_DIM ** -0.5)
        hi, lo = _split2(logf)
        b = _dot(tri, hi) + _dot(tri, lo)
        yield
        b3 = b.reshape(n_chunks, CHUNK, D_REC)
        b_mid = b3[:, CHUNK // 2 - 1:CHUNK // 2, :]
        b_last = b3[:, CHUNK - 1:CHUNK, :]
        qs3 = qs.reshape(n_chunks, CHUNK, D_REC)
        k3 = k.reshape(n_chunks, CHUNK, D_REC)
        q_d = (qs3 * jnp.exp(b3 - b_mid)).reshape(mr, D_REC).astype(BF16)
        k_d = (k3 * jnp.exp(b_mid - b3)).reshape(mr, D_REC).astype(BF16)
        q_in3 = qs3 * jnp.exp(b3)
        k_out3 = k3 * jnp.exp(b_last - b3)
        c = [jnp.zeros((1, D_REC), F32)]
        for n in range(n_chunks):
            c.append(c[n] + b_last[n])
        q_lv = [q_in3.reshape(mr, D_REC).astype(BF16)]
        k_lv = [k_out3.reshape(mr, D_REC).astype(BF16)]
        for l in range(2, n_levels + 1):
            size, half = 1 << l, (1 << l) // 2
            q_rows, k_rows = [], []
            for n in range(n_chunks):
                mid_b = (n // size) * size + half
                q_rows.append(jnp.exp(c[n] - c[mid_b]) if n % size >= half else None)
                k_rows.append(jnp.exp(c[mid_b] - c[n + 1]) if n % size < half else None)
            q_lv.append(_scale_chunks(q_in3, q_rows))
            k_lv.append(_scale_chunks(k_out3, k_rows))
        q_st = _scale_chunks(q_in3, [ones] + [jnp.exp(c[n]) for n in range(1, n_chunks)])
        k_st = _scale_chunks(k_out3, [jnp.exp(c[n_chunks] - c[n + 1]) for n in range(n_chunks - 1)] + [ones])
        sub_decay = jnp.exp(c[n_chunks])
        v = proj_ref[rows, 2 * D_REC:3 * D_REC]
        v_b = v.astype(BF16)
        o_heads = []
        for h in range(N_HEADS):
            hs = slice(h * HEAD_DIM, (h + 1) * HEAD_DIM)
            scores = _dot_nt(q_lv[-1][:, hs], k_lv[-1][:, hs])
            for l in range(n_levels - 2, -1, -1):
                scores = jnp.where(level_masks[l], _dot_nt(q_lv[l][:, hs], k_lv[l][:, hs]), scores)
            scores = jnp.where(causal, _dot_nt(q_d[:, hs], k_d[:, hs]), scores)
            st_t = state_ref[h]
            o_h = _dot(scores.astype(BF16), v_b[:, hs]) + _dot_nt(q_st[:, hs], st_t.astype(BF16))
            v_t = v[:, hs].T.astype(BF16)
            state_ref[h] = st_t * sub_decay[:, hs] + _dot(v_t, k_st[:, hs])
            g_h = proj_ref[rows, 3 * D_REC + h * HEAD_DIM:3 * D_REC + (h + 1) * HEAD_DIM]
            o_heads.append(o_h * _rms_scale([o_h]) * gnw_ref[...] * (g_h * _sigmoid(g_h)))
            if h % 2 == 1:
                yield
        p = proj_ref[rows, 4 * D_REC:4 * D_REC + D_POOL]
        pbuf_ref[W_MAX:W_MAX + mr, :] = p
        pos = t_idx * tt + r * mr + lax.broadcasted_iota(jnp.int32, (mr, POOL_GROUP_DIM), 0)
        for gi, w in enumerate(POOL_WINDOWS):
            cs = slice(gi * POOL_GROUP_DIM, (gi + 1) * POOL_GROUP_DIM)
            win_sum = pbuf_ref[:, cs]
            span = 1
            while span < w:
                win_sum = win_sum + pltpu.roll(win_sum, span, 0)
                span *= 2
            count = jnp.minimum(pos + 1, w).astype(F32)
            pooled = win_sum[W_MAX:, :] * (1.0 / count) - p[:, cs]
            o_heads.append(pooled)
        pbuf_ref[0:W_MAX, :] = pbuf_ref[mr:mr + W_MAX, :]
        mixin_ref[rows, :] = jnp.concatenate(o_heads, axis=1).astype(BF16)
        yield

    def finish():
        x = x_ref[...]
        mix_in = mixin_ref[...]
        for j in range(N_HALF):
            cs = slice(j * DOT_COLS, (j + 1) * DOT_COLS)
            h1_ref[h1_s, :, cs] = x[:, cs] + gt_a[:, cs] * _dot(mix_in, wout_ref[:, cs])
        yield

    stages = {"P": project(), "F": finish(), **{r: mix(r) for r in range(n_sub)}}
    if n_sub == 2:
        plan = ["D0", "P", "U0", "P", 0, "U1", 1, 0, "U2", 0, "D1", "E", 0, 1, 1, "U3", 1, "F"]
    else:
        assert n_sub == 1 and N_UP == 4
        plan = ["D0", "P", "U0", "P", 0, "U1", 0, "U2", 0, "D1", "E", 0, "U3", "F"]
    for action in plan:
        if action in stages:
            if do_mix:
                next(stages[action], None)
        elif action == "E":
            emit_output()
        else:
            (up_proj if action[0] == "U" else down_proj)(int(action[1:]))


def _const_spec(shape):
    zeros = (0,) * len(shape)
    return pl.BlockSpec(shape, lambda *_: zeros, pipeline_mode=pl.Buffered(1))


def kernel(x, c, w_ada, b_ada, norm_mix_w, w_in, lb_logits, g_norm_w, w_pool, pool_scale,
           w_out, norm_mlp_w, w_up, w_down, norm_final_w):
    B, T, D = x.shape
    assert D == D_MODEL and w_ada.shape[0] == 1, "single-layer kernel"
    tt = min(TIME_TILE, T)
    mr = min(MIX_ROWS, tt)
    n_chunks = mr // CHUNK
    assert T % tt == 0 and tt % mr == 0 and mr % CHUNK == 0 and mr >= W_MAX
    assert n_chunks & (n_chunks - 1) == 0 and n_chunks >= 4
    tiles_per_seq = T // tt
    n_tiles = B * tiles_per_seq

    n_mod_cols = N_MOD * D
    mod = pl.pallas_call(
        _mod_kernel,
        grid=(n_mod_cols // MOD_COL_BLOCK,),
        in_specs=[
            pl.BlockSpec((B, D), lambda j: (0, 0)),
            pl.BlockSpec((D, MOD_COL_BLOCK), lambda j: (0, j)),
            pl.BlockSpec((1, MOD_COL_BLOCK), lambda j: (0, j)),
        ],
        out_specs=pl.BlockSpec((B, MOD_COL_BLOCK), lambda j: (0, j)),
        out_shape=jax.ShapeDtypeStruct((B, n_mod_cols), F32),
        name="adaln_mod",
    )(c, w_ada[0], b_ada[0].reshape(1, n_mod_cols))
    mod = mod.reshape(B, N_MOD, D)

    n_groups = len(POOL_WINDOWS)
    w_out_pool = pl.pallas_call(
        _fold_pool_kernel,
        grid=(n_groups,),
        in_specs=[
            pl.BlockSpec((None, POOL_GROUP_DIM, POOL_GROUP_DIM), lambda g: (g, 0, 0)),
            pl.BlockSpec((1, POOL_GROUP_DIM), lambda g: (0, g)),
            pl.BlockSpec((POOL_GROUP_DIM, D), lambda g: (g, 0)),
        ],
        out_specs=pl.BlockSpec((POOL_GROUP_DIM, D), lambda g: (g, 0)),
        out_shape=jax.ShapeDtypeStruct((D_POOL, D), BF16),
        name="fold_pool",
    )(w_pool[0], pool_scale[0].reshape(1, D_POOL), w_out[0][D_REC:, :])
    w_out_eff = jnp.concatenate([w_out[0][:D_REC, :].astype(BF16), w_out_pool], axis=0)

    def mix_tile(s):
        return jnp.minimum(s, n_tiles - 1)

    def up_tile(s):
        return jnp.clip(s - 1, 0, n_tiles - 1)

    def down_tile(s):
        return jnp.maximum(s - 2, 0)

    row = lambda a: a.reshape(1, -1)
    out = pl.pallas_call(
        functools.partial(_layer_kernel, tt=tt, mr=mr, tiles_per_seq=tiles_per_seq, n_tiles=n_tiles),
        grid=(n_tiles + 2,),
        in_specs=[
            pl.BlockSpec((None, tt, D), lambda s: (mix_tile(s) // tiles_per_seq, mix_tile(s) % tiles_per_seq, 0)),
            pl.BlockSpec((None, N_MOD, D), lambda s: (mix_tile(s) // tiles_per_seq, 0, 0)),
            pl.BlockSpec((None, N_MOD, D), lambda s: (up_tile(s) // tiles_per_seq, 0, 0)),
            pl.BlockSpec((None, N_MOD, D), lambda s: (down_tile(s) // tiles_per_seq, 0, 0)),
            _const_spec((1, D)),
            _const_spec((D, D_IN)),
            _const_spec(lb_logits.shape),
            _const_spec((1, HEAD_DIM)),
            _const_spec((D, D)),
            _const_spec((1, D)),
            _const_spec((D, D_FF)),
            _const_spec((D_FF, D)),
            _const_spec((1, D)),
        ],
        out_specs=pl.BlockSpec((None, tt, D),
                               lambda s: (down_tile(s) // tiles_per_seq, down_tile(s) % tiles_per_seq, 0)),
        out_shape=jax.ShapeDtypeStruct((B, T, D), x.dtype),
        scratch_shapes=[
            pltpu.VMEM((N_HEADS, HEAD_DIM, HEAD_DIM), F32),
            pltpu.VMEM((W_MAX + mr, D_POOL), F32),
            pltpu.VMEM((2, tt, D_FF), BF16),
            pltpu.VMEM((2, tt, D), F32),
            pltpu.VMEM((tt, D_IN), F32),
            pltpu.VMEM((tt, D_MODEL), BF16),
        ],
        compiler_params=pltpu.CompilerParams(
            dimension_semantics=("arbitrary",),
            vmem_limit_bytes=VMEM_LIMIT_BYTES,
        ),
        name="hybrid_layer",
    )(x, mod, mod, mod, row(norm_mix_w[0]), w_in[0].astype(BF16), lb_logits, row(g_norm_w[0]),
      w_out_eff, row(norm_mlp_w[0]), w_up[0].astype(BF16), w_down[0].astype(BF16), row(norm_final_w))
    return out
```

```python
import functools

import jax
import jax.numpy as jnp
from jax import lax
from jax.experimental import pallas as pl
from jax.experimental.pallas import tpu as pltpu

F32 = jnp.float32
BF16 = jnp.bfloat16

D_MODEL = 1024
D_REC = 512
D_POOL = 512
HEAD_DIM = 128
N_HEADS = D_REC // HEAD_DIM
POOL_WINDOWS = (2, 4, 8, 16)
POOL_GROUP_DIM = D_POOL // len(POOL_WINDOWS)
W_MAX = max(POOL_WINDOWS)
D_IN = 4 * D_REC + D_POOL
D_FF = 4 * D_MODEL
N_MOD = 6
CHUNK = 32
EPS = 1e-6

TIME_TILE = 512
MIX_ROWS = 256
DOT_COLS = 512
N_IN = D_IN // DOT_COLS
UP_COLS = 1024
N_UP = D_FF // UP_COLS
N_HALF = D_MODEL // DOT_COLS
MOD_COL_BLOCK = 1024
VMEM_LIMIT_BYTES = 63 * 1024 * 1024 + 512 * 1024


def _dot(a, b):
    return jnp.dot(a, b, preferred_element_type=F32)


def _dot_nt(a, b):
    return lax.dot_general(a, b, (((1,), (1,)), ((), ())), preferred_element_type=F32)


def _sigmoid(x):
    return 1.0 / (1.0 + jnp.exp(-x))


def _rms_scale(parts):
    n = sum(a.shape[-1] for a in parts)
    ss = sum(jnp.sum(a * a, axis=-1, keepdims=True) for a in parts)
    return lax.rsqrt(ss * (1.0 / n) + EPS)


def _mod_kernel(c_ref, w_ref, b_ref, o_ref):
    c = c_ref[...]
    c_act = (c * _sigmoid(c)).astype(BF16)
    o_ref[...] = _dot(c_act, w_ref[...].astype(BF16)) + b_ref[...]


def _split2(x):
    hi = x.astype(BF16)
    lo = (x - hi.astype(F32)).astype(BF16)
    return hi, lo


def _fold_pool_kernel(wpool_ref, pscale_ref, wout_ref, o_ref):
    w = wpool_ref[...] * pscale_ref[...]
    o_ref[...] = jnp.dot(w, wout_ref[...], precision=lax.Precision.HIGHEST,
                         preferred_element_type=F32).astype(BF16)


def _scale_chunks(a3, rows):
    n_chunks, c, w = a3.shape
    out = [jnp.zeros((c, w), F32) if r is None else a3[n] * r for n, r in enumerate(rows)]
    return jnp.concatenate(out, axis=0).astype(BF16)


def _layer_kernel(*refs, n_tiles, **static):
    step = pl.program_id(0)
    pl.when(step < n_tiles)(functools.partial(_layer_body, *refs, do_mix=True, n_tiles=n_tiles, **static))
    pl.when(step >= n_tiles)(functools.partial(_layer_body, *refs, do_mix=False, n_tiles=n_tiles, **static))


def _layer_body(x_ref, mod_mix_ref, mod_up_ref, mod_down_ref, nmix_ref, win_ref, lbl_ref, gnw_ref,
                wout_ref, nmlp_ref, wup_ref, wdown_ref, nfin_ref, o_ref,
                state_ref, pbuf_ref, hid_ref, h1_ref, proj_ref, mixin_ref,
                *, do_mix, tt, mr, tiles_per_seq, n_tiles):
    n_sub = tt // mr
    n_chunks = mr // CHUNK
    n_levels = n_chunks.bit_length() - 1
    step = pl.program_id(0)
    t_idx = jnp.minimum(step, n_tiles - 1) % tiles_per_seq
    h1_s, h1_up = step % 2, (step + 1) % 2
    hid_w, hid_r = (step + 1) % 2, step % 2

    if do_mix:
        @pl.when(step == 0)
        def _():
            h1_ref[...] = jnp.zeros_like(h1_ref)
            hid_ref[...] = jnp.zeros_like(hid_ref)

        @pl.when(t_idx == 0)
        def _():
            state_ref[...] = jnp.zeros_like(state_ref)
            pbuf_ref[0:W_MAX, :] = jnp.zeros((W_MAX, D_POOL), F32)

    mod = mod_mix_ref[...]
    sh_a, sc_a, gt_a = (mod[i:i + 1, :] for i in range(3))
    sh_m, sc_m = mod_up_ref[3:4, :], mod_up_ref[4:5, :]
    gt_m_down = mod_down_ref[5:6, :]

    h1_up_tile = h1_ref[h1_up]
    u2_prev = (h1_up_tile * _rms_scale([h1_up_tile]) * nmlp_ref[...] * (1.0 + sc_m) + sh_m).astype(BF16)

    def up_proj(j):
        fs = slice(j * UP_COLS, (j + 1) * UP_COLS)
        a = jnp.maximum(_dot(u2_prev, wup_ref[:, fs]), 0.0)
        hid_ref[hid_w, :, fs] = (a * a).astype(BF16)

    hid_prev = hid_ref[hid_r]
    h2 = []

    def down_proj(j):
        cs = slice(j * DOT_COLS, (j + 1) * DOT_COLS)
        h2.append(h1_ref[h1_s, :, cs] + gt_m_down[:, cs] * _dot(hid_prev, wdown_ref[:, cs]))

    def emit_output():
        scale = _rms_scale(h2)
        for j in range(N_HALF):
            cs = slice(j * DOT_COLS, (j + 1) * DOT_COLS)
            o_ref[:, cs] = h2[j] * scale * nfin_ref[:, cs]

    def project():
        x = x_ref[...]
        u = (x * _rms_scale([x]) * nmix_ref[...] * (1.0 + sc_a) + sh_a).astype(BF16)
        for j in range(N_IN):
            cs = slice(j * DOT_COLS, (j + 1) * DOT_COLS)
            proj_ref[:, cs] = _dot(u, win_ref[:, cs])
            if j == 1:
                yield
        yield

    lbl = lbl_ref[...]
    lbe = jnp.exp(lbl - jnp.max(lbl, axis=0, keepdims=True))
    lb = lbe[0:1, :] / jnp.sum(lbe, axis=0, keepdims=True)
    row = lax.broadcasted_iota(jnp.int32, (mr, mr), 0)
    col = lax.broadcasted_iota(jnp.int32, (mr, mr), 1)
    row_c, col_c = row // CHUNK, col // CHUNK
    causal = (row_c == col_c) & (col <= row)
    tri = jnp.where(causal, 1.0, 0.0).astype(BF16)
    level_masks = [
        (row_c // (1 << l) == col_c // (1 << l)) & (row_c % (1 << l) >= (1 << l) // 2) & (col_c % (1 << l) < (1 << l) // 2)
        for l in range(1, n_levels)]
    ones = jnp.ones((1, D_REC), F32)

    def mix(r):
        rows = slice(r * mr, (r + 1) * mr)
        q = proj_ref[rows, 0 * D_REC:1 * D_REC]
        f_logit = proj_ref[rows, 1 * D_REC:2 * D_REC]
        forget = lb + (1.0 - lb) * _sigmoid(f_logit)
        k = 1.0 - forget
        logf = jnp.log(forget)
        qs = q * _sigmoid(q) * (HEAD_DIM ** -0.5)
        hi, lo = _split2(logf)
        b = _dot(tri, hi) + _dot(tri, lo)
        yield
        b3 = b.reshape(n_chunks, CHUNK, D_REC)
        b_mid = b3[:, CHUNK // 2 - 1:CHUNK // 2, :]
        b_last = b3[:, CHUNK - 1:CHUNK, :]
        qs3 = qs.reshape(n_chunks, CHUNK, D_REC)
        k3 = k.reshape(n_chunks, CHUNK, D_REC)
        q_d = (qs3 * jnp.exp(b3 - b_mid)).reshape(mr, D_REC).astype(BF16)
        k_d = (k3 * jnp.exp(b_mid - b3)).reshape(mr, D_REC).astype(BF16)
        q_in3 = qs3 * jnp.exp(b3)
        k_out3 = k3 * jnp.exp(b_last - b3)
        c = [jnp.zeros((1, D_REC), F32)]
        for n in range(n_chunks):
            c.append(c[n] + b_last[n])
        q_lv = [q_in3.reshape(mr, D_REC).astype(BF16)]
        k_lv = [k_out3.reshape(mr, D_REC).astype(BF16)]
        for l in range(2, n_levels + 1):
            size, half = 1 << l, (1 << l) // 2
            q_rows, k_rows = [], []
            for n in range(n_chunks):
                mid_b = (n // size) * size + half
                q_rows.append(jnp.exp(c[n] - c[mid_b]) if n % size >= half else None)
                k_rows.append(jnp.exp(c[mid_b] - c[n + 1]) if n % size < half else None)
            q_lv.append(_scale_chunks(q_in3, q_rows))
            k_lv.append(_scale_chunks(k_out3, k_rows))
        q_st = _scale_chunks(q_in3, [ones] + [jnp.exp(c[n]) for n in range(1, n_chunks)])
        k_st = _scale_chunks(k_out3, [jnp.exp(c[n_chunks] - c[n + 1]) for n in range(n_chunks - 1)] + [ones])
        sub_decay = jnp.exp(c[n_chunks])
        v = proj_ref[rows, 2 * D_REC:3 * D_REC]
        v_b = v.astype(BF16)
        o_heads = []
        for h in range(N_HEADS):
            hs = slice(h * HEAD_DIM, (h + 1) * HEAD_DIM)
            scores = _dot_nt(q_lv[-1][:, hs], k_lv[-1][:, hs])
            for l in range(n_levels - 2, -1, -1):
                scores = jnp.where(level_masks[l], _dot_nt(q_lv[l][:, hs], k_lv[l][:, hs]), scores)
            scores = jnp.where(causal, _dot_nt(q_d[:, hs], k_d[:, hs]), scores)
            st_t = state_ref[h]
            o_h = _dot(scores.astype(BF16), v_b[:, hs]) + _dot_nt(q_st[:, hs], st_t.astype(BF16))
            v_t = v[:, hs].T.astype(BF16)
            state_ref[h] = st_t * sub_decay[:, hs] + _dot(v_t, k_st[:, hs])
            g_h = proj_ref[rows, 3 * D_REC + h * HEAD_DIM:3 * D_REC + (h + 1) * HEAD_DIM]
            o_heads.append(o_h * _rms_scale([o_h]) * gnw_ref[...] * (g_h * _sigmoid(g_h)))
            if h % 2 == 1:
                yield
        p = proj_ref[rows, 4 * D_REC:4 * D_REC + D_POOL]
        pbuf_ref[W_MAX:W_MAX + mr, :] = p
        pos = t_idx * tt + r * mr + lax.broadcasted_iota(jnp.int32, (mr, POOL_GROUP_DIM), 0)
        for gi, w in enumerate(POOL_WINDOWS):
            cs = slice(gi * POOL_GROUP_DIM, (gi + 1) * POOL_GROUP_DIM)
            win_sum = pbuf_ref[:, cs]
            span = 1
            while span < w:
                win_sum = win_sum + pltpu.roll(win_sum, span, 0)
                span *= 2
            count = jnp.minimum(pos + 1, w).astype(F32)
            pooled = win_sum[W_MAX:, :] * (1.0 / count) - p[:, cs]
            o_heads.append(pooled)
        pbuf_ref[0:W_MAX, :] = pbuf_ref[mr:mr + W_MAX, :]
        mixin_ref[rows, :] = jnp.concatenate(o_heads, axis=1).astype(BF16)
        yield

    def finish():
        x = x_ref[...]
        mix_in = mixin_ref[...]
        for j in range(N_HALF):
            cs = slice(j * DOT_COLS, (j + 1) * DOT_COLS)
            h1_ref[h1_s, :, cs] = x[:, cs] + gt_a[:, cs] * _dot(mix_in, wout_ref[:, cs])
        yield

    stages = {"P": project(), "F": finish(), **{r: mix(r) for r in range(n_sub)}}
    if n_sub == 2:
        plan = ["D0", "P", "U0", "P", 0, "U1", 1, 0, "U2", 0, "D1", "E", 0, 1, 1, 1, "F", "U3"]
    else:
        assert n_sub == 1 and N_UP == 4
        plan = ["D0", "P", "U0", "P", 0, "U1", 0, "U2", 0, "D1", "E", 0, "F", "U3"]
    for action in plan:
        if action in stages:
            if do_mix:
                next(stages[action], None)
        elif action == "E":
            emit_output()
        else:
            (up_proj if action[0] == "U" else down_proj)(int(action[1:]))


def _const_spec(shape):
    zeros = (0,) * len(shape)
    return pl.BlockSpec(shape, lambda *_: zeros, pipeline_mode=pl.Buffered(1))


def kernel(x, c, w_ada, b_ada, norm_mix_w, w_in, lb_logits, g_norm_w, w_pool, pool_scale,
           w_out, norm_mlp_w, w_up, w_down, norm_final_w):
    B, T, D = x.shape
    assert D == D_MODEL and w_ada.shape[0] == 1, "single-layer kernel"
    tt = min(TIME_TILE, T)
    mr = min(MIX_ROWS, tt)
    n_chunks = mr // CHUNK
    assert T % tt == 0 and tt % mr == 0 and mr % CHUNK == 0 and mr >= W_MAX
    assert n_chunks & (n_chunks - 1) == 0 and n_chunks >= 4
    tiles_per_seq = T // tt
    n_tiles = B * tiles_per_seq

    n_mod_cols = N_MOD * D
    mod = pl.pallas_call(
        _mod_kernel,
        grid=(n_mod_cols // MOD_COL_BLOCK,),
        in_specs=[
            pl.BlockSpec((B, D), lambda j: (0, 0)),
            pl.BlockSpec((D, MOD_COL_BLOCK), lambda j: (0, j)),
            pl.BlockSpec((1, MOD_COL_BLOCK), lambda j: (0, j)),
        ],
        out_specs=pl.BlockSpec((B, MOD_COL_BLOCK), lambda j: (0, j)),
        out_shape=jax.ShapeDtypeStruct((B, n_mod_cols), F32),
        name="adaln_mod",
    )(c, w_ada[0], b_ada[0].reshape(1, n_mod_cols))
    mod = mod.reshape(B, N_MOD, D)

    n_groups = len(POOL_WINDOWS)
    w_out_pool = pl.pallas_call(
        _fold_pool_kernel,
        grid=(n_groups,),
        in_specs=[
            pl.BlockSpec((None, POOL_GROUP_DIM, POOL_GROUP_DIM), lambda g: (g, 0, 0)),
            pl.BlockSpec((1, POOL_GROUP_DIM), lambda g: (0, g)),
            pl.BlockSpec((POOL_GROUP_DIM, D), lambda g: (g, 0)),
        ],
        out_specs=pl.BlockSpec((POOL_GROUP_DIM, D), lambda g: (g, 0)),
        out_shape=jax.ShapeDtypeStruct((D_POOL, D), BF16),
        name="fold_pool",
    )(w_pool[0], pool_scale[0].reshape(1, D_POOL), w_out[0][D_REC:, :])
    w_out_eff = jnp.concatenate([w_out[0][:D_REC, :].astype(BF16), w_out_pool], axis=0)

    def mix_tile(s):
        return jnp.minimum(s, n_tiles - 1)

    def up_tile(s):
        return jnp.clip(s - 1, 0, n_tiles - 1)

    def down_tile(s):
        return jnp.maximum(s - 2, 0)

    row = lambda a: a.reshape(1, -1)
    out = pl.pallas_call(
        functools.partial(_layer_kernel, tt=tt, mr=mr, tiles_per_seq=tiles_per_seq, n_tiles=n_tiles),
        grid=(n_tiles + 2,),
        in_specs=[
            pl.BlockSpec((None, tt, D), lambda s: (mix_tile(s) // tiles_per_seq, mix_tile(s) % tiles_per_seq, 0)),
            pl.BlockSpec((None, N_MOD, D), lambda s: (mix_tile(s) // tiles_per_seq, 0, 0)),
            pl.BlockSpec((None, N_MOD, D), lambda s: (up_tile(s) // tiles_per_seq, 0, 0)),
            pl.BlockSpec((None, N_MOD, D), lambda s: (down_tile(s) // tiles_per_seq, 0, 0)),
            _const_spec((1, D)),
            _const_spec((D, D_IN)),
            _const_spec(lb_logits.shape),
            _const_spec((1, HEAD_DIM)),
            _const_spec((D, D)),
            _const_spec((1, D)),
            _const_spec((D, D_FF)),
            _const_spec((D_FF, D)),
            _const_spec((1, D)),
        ],
        out_specs=pl.BlockSpec((None, tt, D),
                               lambda s: (down_tile(s) // tiles_per_seq, down_tile(s) % tiles_per_seq, 0)),
        out_shape=jax.ShapeDtypeStruct((B, T, D), x.dtype),
        scratch_shapes=[
            pltpu.VMEM((N_HEADS, HEAD_DIM, HEAD_DIM), F32),
            pltpu.VMEM((W_MAX + mr, D_POOL), F32),
            pltpu.VMEM((2, tt, D_FF), BF16),
            pltpu.VMEM((2, tt, D), F32),
            pltpu.VMEM((tt, D_IN), F32),
            pltpu.VMEM((tt, D_MODEL), BF16),
        ],
        compiler_params=pltpu.CompilerParams(
            dimension_semantics=("arbitrary",),
            vmem_limit_bytes=VMEM_LIMIT_BYTES,
        ),
        name="hybrid_layer",
    )(x, mod, mod, mod, row(norm_mix_w[0]), w_in[0].astype(BF16), lb_logits, row(g_norm_w[0]),
      w_out_eff, row(norm_mlp_w[0]), w_up[0].astype(BF16), w_down[0].astype(BF16), row(norm_final_w))
    return out
```

```python
import functools

import jax
import jax.numpy as jnp
from jax import lax
from jax.experimental import pallas as pl
from jax.experimental.pallas import tpu as pltpu

F32 = jnp.float32
BF16 = jnp.bfloat16

D_MODEL = 1024
D_REC = 512
D_POOL = 512
HEAD_DIM = 128
N_HEADS = D_REC // HEAD_DIM
POOL_WINDOWS = (2, 4, 8, 16)
POOL_GROUP_DIM = D_POOL // len(POOL_WINDOWS)
W_MAX = max(POOL_WINDOWS)
D_IN = 4 * D_REC + D_POOL
D_FF = 4 * D_MODEL
N_MOD = 6
CHUNK = 32
EPS = 1e-6

TIME_TILE = 512
MIX_ROWS = 256
DOT_COLS = 512
N_IN = D_IN // DOT_COLS
UP_COLS = 1024
N_UP = D_FF // UP_COLS
N_HALF = D_MODEL // DOT_COLS
MOD_COL_BLOCK = 1024
V7X_VMEM_BYTES = 64 * 1024 * 1024
VMEM_LIMIT_BYTES = V7X_VMEM_BYTES - 512 * 1024


def _dot(a, b):
    return jnp.dot(a, b, preferred_element_type=F32)


def _dot_nt(a, b):
    return lax.dot_general(a, b, (((1,), (1,)), ((), ())), preferred_element_type=F32)


def _sigmoid(x):
    return 1.0 / (1.0 + jnp.exp(-x))


def _rms_scale(parts):
    n = sum(a.shape[-1] for a in parts)
    ss = sum(jnp.sum(a * a, axis=-1, keepdims=True) for a in parts)
    return lax.rsqrt(ss * (1.0 / n) + EPS)


def _mod_kernel(c_ref, w_ref, b_ref, o_ref):
    c = c_ref[...]
    c_act = (c * _sigmoid(c)).astype(BF16)
    o_ref[...] = _dot(c_act, w_ref[...].astype(BF16)) + b_ref[...]


def _split2(x):
    hi = x.astype(BF16)
    lo = (x - hi.astype(F32)).astype(BF16)
    return hi, lo


def _fold_pool_kernel(wpool_ref, pscale_ref, wout_ref, o_ref):
    w = wpool_ref[...] * pscale_ref[...]
    o_ref[...] = jnp.dot(w, wout_ref[...], precision=lax.Precision.HIGHEST,
                         preferred_element_type=F32).astype(BF16)


def _scale_chunks(a3, rows):
    n_chunks, c, w = a3.shape
    out = [jnp.zeros((c, w), F32) if r is None else a3[n] * r for n, r in enumerate(rows)]
    return jnp.concatenate(out, axis=0).astype(BF16)


def _layer_kernel(*refs, n_tiles, **static):
    step = pl.program_id(0)
    pl.when(step < n_tiles)(functools.partial(_layer_body, *refs, do_mix=True, n_tiles=n_tiles, **static))
    pl.when(step >= n_tiles)(functools.partial(_layer_body, *refs, do_mix=False, n_tiles=n_tiles, **static))


def _layer_body(x_ref, mod_mix_ref, mod_up_ref, mod_down_ref, nmix_ref, win_ref, lbl_ref, gnw_ref,
                wout_ref, nmlp_ref, wup_ref, wdown_ref, nfin_ref, o_ref,
                state_ref, pbuf_ref, hid_ref, h1_ref, proj_ref, mixin_ref,
                *, do_mix, tt, mr, tiles_per_seq, n_tiles):
    n_sub = tt // mr
    n_chunks = mr // CHUNK
    n_levels = n_chunks.bit_length() - 1
    step = pl.program_id(0)
    t_idx = jnp.minimum(step, n_tiles - 1) % tiles_per_seq
    h1_s, h1_up = step % 2, (step + 1) % 2
    hid_w, hid_r = (step + 1) % 2, step % 2

    if do_mix:
        @pl.when(step == 0)
        def _():
            h1_ref[...] = jnp.zeros_like(h1_ref)
            hid_ref[...] = jnp.zeros_like(hid_ref)

        @pl.when(t_idx == 0)
        def _():
            state_ref[...] = jnp.zeros_like(state_ref)
            pbuf_ref[0:W_MAX, :] = jnp.zeros((W_MAX, D_POOL), F32)

    mod = mod_mix_ref[...]
    sh_a, sc_a, gt_a = (mod[i:i + 1, :] for i in range(3))
    sh_m, sc_m = mod_up_ref[3:4, :], mod_up_ref[4:5, :]
    gt_m_down = mod_down_ref[5:6, :]

    h1_up_tile = h1_ref[h1_up]
    u2_prev = (h1_up_tile * _rms_scale([h1_up_tile]) * nmlp_ref[...] * (1.0 + sc_m) + sh_m).astype(BF16)

    def up_proj(j):
        fs = slice(j * UP_COLS, (j + 1) * UP_COLS)
        a = jnp.maximum(_dot(u2_prev, wup_ref[:, fs]), 0.0)
        hid_ref[hid_w, :, fs] = (a * a).astype(BF16)

    hid_prev = hid_ref[hid_r]
    h2 = []

    def down_proj(j):
        cs = slice(j * DOT_COLS, (j + 1) * DOT_COLS)
        h2.append(h1_ref[h1_s, :, cs] + gt_m_down[:, cs] * _dot(hid_prev, wdown_ref[:, cs]))

    def emit_output():
        scale = _rms_scale(h2)
        for j in range(N_HALF):
            cs = slice(j * DOT_COLS, (j + 1) * DOT_COLS)
            o_ref[:, cs] = h2[j] * scale * nfin_ref[:, cs]

    def project():
        x = x_ref[...]
        u = (x * _rms_scale([x]) * nmix_ref[...] * (1.0 + sc_a) + sh_a).astype(BF16)
        for j in range(N_IN):
            cs = slice(j * DOT_COLS, (j + 1) * DOT_COLS)
            proj_ref[:, cs] = _dot(u, win_ref[:, cs])
            if j == 1:
                yield
        yield

    lbl = lbl_ref[...]
    lbe = jnp.exp(lbl - jnp.max(lbl, axis=0, keepdims=True))
    lb = lbe[0:1, :] / jnp.sum(lbe, axis=0, keepdims=True)
    row = lax.broadcasted_iota(jnp.int32, (mr, mr), 0)
    col = lax.broadcasted_iota(jnp.int32, (mr, mr), 1)
    row_c, col_c = row // CHUNK, col // CHUNK
    causal = (row_c == col_c) & (col <= row)
    tri = jnp.where(causal, 1.0, 0.0).astype(BF16)
    level_masks = [
        (row_c // (1 << l) == col_c // (1 << l)) & (row_c % (1 << l) >= (1 << l) // 2) & (col_c % (1 << l) < (1 << l) // 2)
        for l in range(1, n_levels)]
    ones = jnp.ones((1, D_REC), F32)

    def mix(r):
        rows = slice(r * mr, (r + 1) * mr)
        q = proj_ref[rows, 0 * D_REC:1 * D_REC]
        f_logit = proj_ref[rows, 1 * D_REC:2 * D_REC]
        forget = lb + (1.0 - lb) * _sigmoid(f_logit)
        k = 1.0 - forget
        logf = jnp.log(forget)
        qs = q * _sigmoid(q) * (HEAD_DIM ** -0.5)
        hi, lo = _split2(logf)
        b = _dot(tri, hi) + _dot(tri, lo)
        yield
        b3 = b.reshape(n_chunks, CHUNK, D_REC)
        b_mid = b3[:, CHUNK // 2 - 1:CHUNK // 2, :]
        b_last = b3[:, CHUNK - 1:CHUNK, :]
        qs3 = qs.reshape(n_chunks, CHUNK, D_REC)
        k3 = k.reshape(n_chunks, CHUNK, D_REC)
        q_d = (qs3 * jnp.exp(b3 - b_mid)).reshape(mr, D_REC).astype(BF16)
        k_d = (k3 * jnp.exp(b_mid - b3)).reshape(mr, D_REC).astype(BF16)
        q_in3 = qs3 * jnp.exp(b3)
        k_out3 = k3 * jnp.exp(b_last - b3)
        c = [jnp.zeros((1, D_REC), F32)]
        for n in range(n_chunks):
            c.append(c[n] + b_last[n])
        q_lv = [q_in3.reshape(mr, D_REC).astype(BF16)]
        k_lv = [k_out3.reshape(mr, D_REC).astype(BF16)]
        for l in range(2, n_levels + 1):
            size, half = 1 << l, (1 << l) // 2
            q_rows, k_rows = [], []
            for n in range(n_chunks):
                mid_b = (n // size) * size + half
                q_rows.append(jnp.exp(c[n] - c[mid_b]) if n % size >= half else None)
                k_rows.append(jnp.exp(c[mid_b] - c[n + 1]) if n % size < half else None)
            q_lv.append(_scale_chunks(q_in3, q_rows))
            k_lv.append(_scale_chunks(k_out3, k_rows))
        q_st = _scale_chunks(q_in3, [ones] + [jnp.exp(c[n]) for n in range(1, n_chunks)])
        k_st = _scale_chunks(k_out3, [jnp.exp(c[n_chunks] - c[n + 1]) for n in range(n_chunks - 1)] + [ones])
        sub_decay = jnp.exp(c[n_chunks])
        v = proj_ref[rows, 2 * D_REC:3 * D_REC]
        v_b = v.astype(BF16)
        o_heads = []
        for h in range(N_HEADS):
            hs = slice(h * HEAD_DIM, (h + 1) * HEAD_DIM)
            scores = _dot_nt(q_lv[-1][:, hs], k_lv[-1][:, hs])
            for l in range(n_levels - 2, -1, -1):
                scores = jnp.where(level_masks[l], _dot_nt(q_lv[l][:, hs], k_lv[l][:, hs]), scores)
            scores = jnp.where(causal, _dot_nt(q_d[:, hs], k_d[:, hs]), scores)
            st_t = state_ref[h]
            o_h = _dot(scores.astype(BF16), v_b[:, hs]) + _dot_nt(q_st[:, hs], st_t.astype(BF16))
            v_t = v[:, hs].T.astype(BF16)
            state_ref[h] = st_t * sub_decay[:, hs] + _dot(v_t, k_st[:, hs])
            g_h = proj_ref[rows, 3 * D_REC + h * HEAD_DIM:3 * D_REC + (h + 1) * HEAD_DIM]
            o_heads.append(o_h * _rms_scale([o_h]) * gnw_ref[...] * (g_h * _sigmoid(g_h)))
            if h % 2 == 1:
                yield
        p = proj_ref[rows, 4 * D_REC:4 * D_REC + D_POOL]
        pbuf_ref[W_MAX:W_MAX + mr, :] = p
        pos = t_idx * tt + r * mr + lax.broadcasted_iota(jnp.int32, (mr, POOL_GROUP_DIM), 0)
        for gi, w in enumerate(POOL_WINDOWS):
            cs = slice(gi * POOL_GROUP_DIM, (gi + 1) * POOL_GROUP_DIM)
            win_sum = pbuf_ref[:, cs]
            span = 1
            while span < w:
                win_sum = win_sum + pltpu.roll(win_sum, span, 0)
                span *= 2
            count = jnp.minimum(pos + 1, w).astype(F32)
            pooled = win_sum[W_MAX:, :] * (1.0 / count) - p[:, cs]
            o_heads.append(pooled)
        pbuf_ref[0:W_MAX, :] = pbuf_ref[mr:mr + W_MAX, :]
        mixin_ref[rows, :] = jnp.concatenate(o_heads, axis=1).astype(BF16)
        yield

    def finish():
        x = x_ref[...]
        mix_in = mixin_ref[...]
        for j in range(N_HALF):
            cs = slice(j * DOT_COLS, (j + 1) * DOT_COLS)
            h1_ref[h1_s, :, cs] = x[:, cs] + gt_a[:, cs] * _dot(mix_in, wout_ref[:, cs])
        yield

    stages = {"P": project(), "F": finish(), **{r: mix(r) for r in range(n_sub)}}
    if n_sub == 2:
        plan = ["D0", "P", "U0", "P", 0, "U1", 1, 0, "U2", 0, "D1", "E", 0, 1, 1, "U3", 1, "F"]
    else:
        assert n_sub == 1 and N_UP == 4
        plan = ["D0", "P", "U0", "P", 0, "U1", 0, "U2", 0, "D1", "E", 0, "U3", "F"]
    for action in plan:
        if action in stages:
            if do_mix:
                next(stages[action], None)
        elif action == "E":
            emit_output()
        else:
            (up_proj if action[0] == "U" else down_proj)(int(action[1:]))


def _const_spec(shape):
    zeros = (0,) * len(shape)
    return pl.BlockSpec(shape, lambda *_: zeros, pipeline_mode=pl.Buffered(1))


def kernel(x, c, w_ada, b_ada, norm_mix_w, w_in, lb_logits, g_norm_w, w_pool, pool_scale,
           w_out, norm_mlp_w, w_up, w_down, norm_final_w):
    B, T, D = x.shape
    assert D == D_MODEL and w_ada.shape[0] == 1, "single-layer kernel"
    tt = min(TIME_TILE, T)
    mr = min(MIX_ROWS, tt)
    n_chunks = mr // CHUNK
    assert T % tt == 0 and tt % mr == 0 and mr % CHUNK == 0 and mr >= W_MAX
    assert n_chunks & (n_chunks - 1) == 0 and n_chunks >= 4
    tiles_per_seq = T // tt
    n_tiles = B * tiles_per_seq

    n_mod_cols = N_MOD * D
    mod = pl.pallas_call(
        _mod_kernel,
        grid=(n_mod_cols // MOD_COL_BLOCK,),
        in_specs=[
            pl.BlockSpec((B, D), lambda j: (0, 0)),
            pl.BlockSpec((D, MOD_COL_BLOCK), lambda j: (0, j)),
            pl.BlockSpec((1, MOD_COL_BLOCK), lambda j: (0, j)),
        ],
        out_specs=pl.BlockSpec((B, MOD_COL_BLOCK), lambda j: (0, j)),
        out_shape=jax.ShapeDtypeStruct((B, n_mod_cols), F32),
        name="adaln_mod",
    )(c, w_ada[0], b_ada[0].reshape(1, n_mod_cols))
    mod = mod.reshape(B, N_MOD, D)

    n_groups = len(POOL_WINDOWS)
    w_out_pool = pl.pallas_call(
        _fold_pool_kernel,
        grid=(n_groups,),
        in_specs=[
            pl.BlockSpec((None, POOL_GROUP_DIM, POOL_GROUP_DIM), lambda g: (g, 0, 0)),
            pl.BlockSpec((1, POOL_GROUP_DIM), lambda g: (0, g)),
            pl.BlockSpec((POOL_GROUP_DIM, D), lambda g: (g, 0)),
        ],
        out_specs=pl.BlockSpec((POOL_GROUP_DIM, D), lambda g: (g, 0)),
        out_shape=jax.ShapeDtypeStruct((D_POOL, D), BF16),
        name="fold_pool",
    )(w_pool[0], pool_scale[0].reshape(1, D_POOL), w_out[0][D_REC:, :])
    w_out_eff = jnp.concatenate([w_out[0][:D_REC, :].astype(BF16), w_out_pool], axis=0)

    def mix_tile(s):
        return jnp.minimum(s, n_tiles - 1)

    def up_tile(s):
        return jnp.clip(s - 1, 0, n_tiles - 1)

    def down_tile(s):
        return jnp.maximum(s - 2, 0)

    row = lambda a: a.reshape(1, -1)
    out = pl.pallas_call(
        functools.partial(_layer_kernel, tt=tt, mr=mr, tiles_per_seq=tiles_per_seq, n_tiles=n_tiles),
        grid=(n_tiles + 2,),
        in_specs=[
            pl.BlockSpec((None, tt, D), lambda s: (mix_tile(s) // tiles_per_seq, mix_tile(s) % tiles_per_seq, 0)),
            pl.BlockSpec((None, N_MOD, D), lambda s: (mix_tile(s) // tiles_per_seq, 0, 0)),
            pl.BlockSpec((None, N_MOD, D), lambda s: (up_tile(s) // tiles_per_seq, 0, 0)),
            pl.BlockSpec((None, N_MOD, D), lambda s: (down_tile(s) // tiles_per_seq, 0, 0)),
            _const_spec((1, D)),
            _const_spec((D, D_IN)),
            _const_spec(lb_logits.shape),
            _const_spec((1, HEAD_DIM)),
            _const_spec((D, D)),
            _const_spec((1, D)),
            _const_spec((D, D_FF)),
            _const_spec((D_FF, D)),
            _const_spec((1, D)),
        ],
        out_specs=pl.BlockSpec((None, tt, D),
                               lambda s: (down_tile(s) // tiles_per_seq, down_tile(s) % tiles_per_seq, 0)),
        out_shape=jax.ShapeDtypeStruct((B, T, D), x.dtype),
        scratch_shapes=[
            pltpu.VMEM((N_HEADS, HEAD_DIM, HEAD_DIM), F32),
            pltpu.VMEM((W_MAX + mr, D_POOL), F32),
            pltpu.VMEM((2, tt, D_FF), BF16),
            pltpu.VMEM((2, tt, D), F32),
            pltpu.VMEM((tt, D_IN), F32),
            pltpu.VMEM((tt, D_MODEL), BF16),
        ],
        compiler_params=pltpu.CompilerParams(
            dimension_semantics=("arbitrary",),
            vmem_limit_bytes=VMEM_LIMIT_BYTES,
        ),
        name="hybrid_layer",
    )(x, mod, mod, mod, row(norm_mix_w[0]), w_in[0].astype(BF16), lb_logits, row(g_norm_w[0]),
      w_out_eff, row(norm_mlp_w[0]), w_up[0].astype(BF16), w_down[0].astype(BF16), row(norm_final_w))
    return out
```
